```python
import math
import jax, jax.numpy as jnp
from jax import lax
import numpy as np

D_MODEL = 1024
BATCH = 16
SEQ = 2048
DEPTH = 4

HEAD_DIM = 64
SWA_Q_HEADS = 8
SWA_KV_HEADS = 2
SWA_GROUP = SWA_Q_HEADS // SWA_KV_HEADS
SWA_WINDOW = 128
SB_HEADS = 8
BLOCK = 128
N_EXPERTS = 32
N_GROUPS = 8
EXPERTS_PER_GROUP = N_EXPERTS // N_GROUPS
TOP_K = 2
D_EXPERT = 512
MOE_BLOCK = 128
LN_EPS = 1e-5
DEEPNORM_ALPHA = (2 * DEPTH) ** 0.25
DEEPNORM_BETA = (8 * DEPTH) ** -0.25
PROJ_SIZES = (SWA_Q_HEADS * HEAD_DIM, SWA_KV_HEADS * HEAD_DIM, SWA_KV_HEADS * HEAD_DIM,
              SB_HEADS * HEAD_DIM, SB_HEADS * HEAD_DIM, SB_HEADS * HEAD_DIM,
              D_MODEL, D_MODEL)
VALUE_SLOTS = (2, 5)

kernel_name = "hybrid_swa_stickbreak_groupmoe_deepnorm"


def layer_norm(x, g, b):
    xf = x.astype(jnp.float32)
    mu = xf.mean(-1, keepdims=True)
    var = jnp.square(xf - mu).mean(-1, keepdims=True)
    y = (xf - mu) * lax.rsqrt(var + LN_EPS) * g.astype(jnp.float32) + b.astype(jnp.float32)
    return y.astype(x.dtype)


def alibi_slopes(n_heads):
    return jnp.exp2(-8.0 * (jnp.arange(n_heads, dtype=jnp.float32) + 1.0) / n_heads)


def sliding_window_attention(q, k, v, sinks):
    B, S = q.shape[0], q.shape[1]
    nb = S // BLOCK
    qb = q.reshape(B, nb, BLOCK, SWA_KV_HEADS, SWA_GROUP, HEAD_DIM).astype(jnp.float32)

    def with_prev(t):
        tb = t.reshape(B, nb, BLOCK, SWA_KV_HEADS, HEAD_DIM)
        prev = jnp.pad(tb[:, :-1], ((0, 0), (1, 0), (0, 0), (0, 0), (0, 0)))
        return jnp.concatenate([prev, tb], axis=2).astype(jnp.float32)

    kb, vb = with_prev(k), with_prev(v)
    scores = jnp.einsum('bnqhgd,bnkhd->bnhgqk', qb, kb) * (HEAD_DIM ** -0.5)
    i = jnp.arange(BLOCK)[:, None]
    j = jnp.arange(2 * BLOCK)[None, :]
    dist = i + BLOCK - j
    blk = jnp.arange(nb)[:, None, None]
    valid = (dist >= 0) & (dist < SWA_WINDOW) & ((blk - 1) * BLOCK + j[None] >= 0)
    slopes = alibi_slopes(SWA_Q_HEADS).reshape(SWA_KV_HEADS, SWA_GROUP)
    scores = scores - slopes[:, :, None, None] * dist.astype(jnp.float32)
    scores = jnp.where(valid[None, :, None, None], scores, -jnp.inf)
    sink = sinks.astype(jnp.float32).reshape(SWA_KV_HEADS, SWA_GROUP)[:, :, None, None]
    m = jnp.maximum(scores.max(-1, keepdims=True), sink)
    p = jnp.exp(scores - m)
    probs = p / (p.sum(-1, keepdims=True) + jnp.exp(sink - m))
    out = jnp.einsum('bnhgqk,bnkhd->bnqhgd', probs, vb)
    return out.reshape(B, S, SWA_Q_HEADS * HEAD_DIM)


def stick_breaking_attention(q, k, v):
    B, S = q.shape[0], q.shape[1]
    qf, kf, vf = q.astype(jnp.float32), k.astype(jnp.float32), v.astype(jnp.float32)
    outs = []
    for t0 in range(0, S, BLOCK):
        n_keys = t0 + BLOCK
        z = jnp.einsum('bqhd,bkhd->bhqk', qf[:, t0:t0 + BLOCK], kf[:, :n_keys]) * (HEAD_DIM ** -0.5)
        strict = jnp.arange(n_keys)[None, :] < (t0 + jnp.arange(BLOCK))[:, None]
        log_keep = jnp.where(strict, jax.nn.log_sigmoid(-z), 0.0)
        between = lax.cumsum(log_keep, axis=3, reverse=True) - log_keep
        a = jnp.where(strict, jnp.exp(jax.nn.log_sigmoid(z) + between), 0.0)
        outs.append(jnp.einsum('bhqk,bkhd->bqhd', a, vf[:, :n_keys]))
    return jnp.concatenate(outs, axis=1).reshape(B, S, SB_HEADS * HEAD_DIM)


def token_mixer(x, w_in, b_in, sinks, w_branch_a, w_branch_b, w_out):
    B, S, _ = x.shape
    proj = jnp.einsum('bsd,dn->bsn', x, w_in) + b_in
    cuts, acc = [], 0
    for n in PROJ_SIZES[:-1]:
        acc += n
        cuts.append(acc)
    a_q, a_k, a_v, s_q, s_k, s_v, g_a, g_b = jnp.split(proj, cuts, axis=-1)
    y_a = sliding_window_attention(a_q.reshape(B, S, SWA_Q_HEADS, HEAD_DIM),
                                   a_k.reshape(B, S, SWA_KV_HEADS, HEAD_DIM),
                                   a_v.reshape(B, S, SWA_KV_HEADS, HEAD_DIM), sinks).astype(x.dtype)
    y_b = stick_breaking_attention(s_q.reshape(B, S, SB_HEADS, HEAD_DIM),
                                   s_k.reshape(B, S, SB_HEADS, HEAD_DIM),
                                   s_v.reshape(B, S, SB_HEADS, HEAD_DIM)).astype(x.dtype)
    merged = (jax.nn.sigmoid(g_a) * jnp.einsum('bsc,cd->bsd', y_a, w_branch_a)
              + jax.nn.sigmoid(g_b) * jnp.einsum('bsc,cd->bsd', y_b, w_branch_b))
    return jnp.einsum('bsd,de->bse', merged, w_out)


def group_route(x2d, w_router, router_bias):
    T = x2d.shape[0]
    aff = jax.nn.sigmoid(x2d.astype(jnp.float32) @ w_router.astype(jnp.float32))
    biased = (aff + router_bias.astype(jnp.float32)).reshape(T, N_GROUPS, EXPERTS_PER_GROUP)
    group_score = lax.top_k(biased, TOP_K)[0].sum(-1)
    g_sel = jnp.argmax(group_score, axis=-1)
    in_group = jnp.take_along_axis(biased, g_sel[:, None, None], axis=1)[:, 0]
    _, local = lax.top_k(in_group, TOP_K)
    idx = g_sel[:, None] * EXPERTS_PER_GROUP + local
    w = jnp.take_along_axis(aff, idx, axis=1)
    return idx, w / w.sum(-1, keepdims=True)


def moe(x, w_router, router_bias, w_gate, w_up, w_down):
    B, S, D = x.shape
    T = B * S
    x2d = x.reshape(T, D)
    idx, gates = group_route(x2d, w_router, router_bias)
    flat_e = idx.reshape(-1)
    flat_tok = jnp.repeat(jnp.arange(T, dtype=jnp.int32), TOP_K)
    flat_gate = gates.reshape(-1)
    order = jnp.argsort(flat_e)
    sorted_e = flat_e[order]
    counts = jnp.zeros((N_EXPERTS,), jnp.int32).at[flat_e].add(1)
    padded = ((counts + MOE_BLOCK - 1) // MOE_BLOCK) * MOE_BLOCK
    pad_end = jnp.cumsum(padded)
    pad_start = pad_end - padded
    start = jnp.cumsum(counts) - counts
    rank = jnp.arange(T * TOP_K) - start[sorted_e]
    dest = pad_start[sorted_e] + rank
    n_blocks = -(-(T * TOP_K) // MOE_BLOCK) + N_EXPERTS
    P = n_blocks * MOE_BLOCK
    buf_tok = jnp.full((P,), T, jnp.int32).at[dest].set(flat_tok[order])
    buf_gate = jnp.zeros((P,), jnp.float32).at[dest].set(flat_gate[order])
    block_expert = jnp.minimum(
        jnp.searchsorted(pad_end, jnp.arange(n_blocks) * MOE_BLOCK, side='right'), N_EXPERTS - 1)
    xpad = jnp.concatenate([x2d, jnp.zeros((1, D), x2d.dtype)], axis=0)
    xblocks = xpad[buf_tok].reshape(n_blocks, MOE_BLOCK, D)

    def expert_block(args):
        xb, e = args
        h = jax.nn.silu(xb @ w_gate[e]) * (xb @ w_up[e])
        return h @ w_down[e]

    yb = lax.map(expert_block, (xblocks, block_expert)).reshape(P, D)
    y = jax.ops.segment_sum(yb.astype(jnp.float32) * buf_gate[:, None], buf_tok, num_segments=T + 1)[:T]
    return y.astype(x.dtype).reshape(B, S, D)


def setup_inputs(seed: int = 0) -> dict:
    key = jax.random.key(seed)
    ks = jax.random.split(key, 16)
    n_in = sum(PROJ_SIZES)
    col_scale = jnp.concatenate([
        jnp.full((n,), DEEPNORM_BETA if i in VALUE_SLOTS else 1.0, jnp.float32)
        for i, n in enumerate(PROJ_SIZES)])
    d_a = SWA_Q_HEADS * HEAD_DIM
    d_b = SB_HEADS * HEAD_DIM
    nrm = lambda k, shape: jax.random.normal(k, shape, jnp.float32)
    return {
        "x": nrm(ks[0], (BATCH, SEQ, D_MODEL)),
        "w_in": nrm(ks[1], (DEPTH, D_MODEL, n_in)) * (D_MODEL ** -0.5) * col_scale,
        "b_in": 0.02 * nrm(ks[2], (DEPTH, n_in)),
        "attn_sinks": nrm(ks[3], (DEPTH, SWA_Q_HEADS)),
        "w_branch_a": nrm(ks[4], (DEPTH, d_a, D_MODEL)) * (d_a ** -0.5) * DEEPNORM_BETA,
        "w_branch_b": nrm(ks[5], (DEPTH, d_b, D_MODEL)) * (d_b ** -0.5) * DEEPNORM_BETA,
        "w_out": nrm(ks[6], (DEPTH, D_MODEL, D_MODEL)) * (D_MODEL ** -0.5) * DEEPNORM_BETA,
        "ln1_g": 1.0 + 0.02 * nrm(ks[7], (DEPTH, D_MODEL)),
        "ln1_b": 0.02 * nrm(ks[8], (DEPTH, D_MODEL)),
        "w_router": nrm(ks[9], (D_MODEL, N_EXPERTS)) * (D_MODEL ** -0.5),
        "router_bias": 0.01 * nrm(ks[10], (N_EXPERTS,)),
        "w_gate": nrm(ks[11], (DEPTH, N_EXPERTS, D_MODEL, D_EXPERT)) * (D_MODEL ** -0.5),
        "w_up": nrm(ks[12], (DEPTH, N_EXPERTS, D_MODEL, D_EXPERT)) * (D_MODEL ** -0.5) * DEEPNORM_BETA,
        "w_down": nrm(ks[13], (DEPTH, N_EXPERTS, D_EXPERT, D_MODEL)) * (D_EXPERT ** -0.5) * DEEPNORM_BETA,
        "ln2_g": 1.0 + 0.02 * nrm(ks[14], (DEPTH, D_MODEL)),
        "ln2_b": 0.02 * nrm(ks[15], (DEPTH, D_MODEL)),
    }


def reference(x, w_in, b_in, attn_sinks, w_branch_a, w_branch_b, w_out, ln1_g, ln1_b,
              w_router, router_bias, w_gate, w_up, w_down, ln2_g, ln2_b):
    for l in range(DEPTH):
        mix = token_mixer(x, w_in[l], b_in[l], attn_sinks[l], w_branch_a[l], w_branch_b[l], w_out[l])
        x = layer_norm(DEEPNORM_ALPHA * x + mix, ln1_g[l], ln1_b[l])
        ffn = moe(x, w_router, router_bias, w_gate[l], w_up[l], w_down[l])
        x = layer_norm(DEEPNORM_ALPHA * x + ffn, ln2_g[l], ln2_b[l])
    return x
```

```python
import functools

import jax
import jax.numpy as jnp
from jax import lax
from jax.experimental import pallas as pl
from jax.experimental.pallas import tpu as pltpu

D_MODEL = 1024
BATCH = 16
SEQ = 2048
DEPTH = 4
HEAD_DIM = 64
SWA_Q_HEADS = 8
SWA_KV_HEADS = 2
SWA_GROUP = SWA_Q_HEADS // SWA_KV_HEADS
SB_HEADS = 8
BLOCK = 128
N_EXPERTS = 32
N_GROUPS = 8
EXPERTS_PER_GROUP = N_EXPERTS // N_GROUPS
TOP_K = 2
D_EXPERT = 512
LN_EPS = 1e-5
DEEPNORM_ALPHA = (2 * DEPTH) ** 0.25

TOKENS = BATCH * SEQ
N_QBLOCKS = SEQ // BLOCK
LANES = 128
D_SWA_Q = SWA_Q_HEADS * HEAD_DIM
D_SWA_KV_DUP = 2 * SWA_KV_HEADS * HEAD_DIM
D_SB = SB_HEADS * HEAD_DIM
QKV_COLS = D_SWA_Q + 2 * D_SWA_KV_DUP + 3 * D_SB

ROW_BLOCK = 512
SB_KEY_TILE = 256
EXPERT_ROWS = 512
N_EXPERT_BLOCKS = (TOKENS * TOP_K) // EXPERT_ROWS + N_EXPERTS
GATHER_ROWS = 256
VMEM_LIMIT = 56 * 1024 * 1024

BF16 = jnp.bfloat16
F32 = jnp.float32
NT_DIMS = (((1,), (1,)), ((), ()))


def _params(n_axes, vmem=VMEM_LIMIT):
    return pltpu.CompilerParams(dimension_semantics=("arbitrary",) * n_axes,
                                vmem_limit_bytes=vmem)


def _layer_norm(h, g, b):
    mu = jnp.mean(h, axis=-1, keepdims=True)
    c = h - mu
    var = jnp.mean(c * c, axis=-1, keepdims=True)
    return c * lax.rsqrt(var + LN_EPS) * g + b


_QKV_SLICES = ((0, 512), (512, 768), (768, 1024), (1024, 1536), (1536, 2048), (2048, 2560))


def _inproj_kernel(x_ref, w_ref, b_ref, *out_refs):
    xb = x_ref[...].astype(BF16)
    for (lo, hi), o_ref in zip(_QKV_SLICES, out_refs):
        p = jnp.dot(xb, w_ref[:, lo:hi], preferred_element_type=F32) + b_ref[:, lo:hi]
        o_ref[...] = p.astype(BF16)


def _inproj(x2, w_qkv, b_qkv):
    widths = [hi - lo for lo, hi in _QKV_SLICES]
    return pl.pallas_call(
        _inproj_kernel,
        grid=(TOKENS // ROW_BLOCK,),
        in_specs=[
            pl.BlockSpec((ROW_BLOCK, D_MODEL), lambda i: (i, 0)),
            pl.BlockSpec((D_MODEL, QKV_COLS), lambda i: (0, 0)),
            pl.BlockSpec((1, QKV_COLS), lambda i: (0, 0)),
        ],
        out_specs=[pl.BlockSpec((ROW_BLOCK, w), lambda i: (i, 0)) for w in widths],
        out_shape=[jax.ShapeDtypeStruct((TOKENS, w), BF16) for w in widths],
        compiler_params=_params(1),
        name="inproj",
    )(x2, w_qkv, b_qkv)


def _swa_kernel(sink_ref, q_ref, kp_ref, kc_ref, vp_ref, vc_ref, o_ref):
    n = pl.program_id(1)
    lane = lax.broadcasted_iota(jnp.int32, (BLOCK, LANES), 1)
    low_half = lane < HEAD_DIM
    qi = lax.broadcasted_iota(jnp.int32, (BLOCK, 2 * BLOCK), 0)
    kj = lax.broadcasted_iota(jnp.int32, (BLOCK, 2 * BLOCK), 1)
    dist = qi + BLOCK - kj
    valid = (dist >= 0) & (dist < BLOCK) & ((kj >= BLOCK) | (n > 0))
    dist_f = dist.astype(F32)
    zero = jnp.zeros((), BF16)
    for grp in range(D_SWA_Q // LANES):
        kvh = (2 * grp) // SWA_GROUP
        q = q_ref[:, grp * LANES:(grp + 1) * LANES]
        k = jnp.concatenate([kp_ref[:, kvh * LANES:(kvh + 1) * LANES],
                             kc_ref[:, kvh * LANES:(kvh + 1) * LANES]], axis=0)
        v = jnp.concatenate([vp_ref[:, kvh * LANES:(kvh + 1) * LANES],
                             vc_ref[:, kvh * LANES:(kvh + 1) * LANES]], axis=0)
        outs = []
        for half in range(2):
            head = 2 * grp + half
            slope = 2.0 ** (-8.0 * (head + 1) / SWA_Q_HEADS)
            sink = sink_ref[head]
            qm = jnp.where(low_half if half == 0 else jnp.logical_not(low_half), q, zero)
            s = lax.dot_general(qm, k, NT_DIMS, preferred_element_type=F32)
            s = s - slope * dist_f
            s = jnp.where(valid, s, -jnp.inf)
            m = jnp.maximum(jnp.max(s, axis=-1, keepdims=True), sink)
            p = jnp.exp(s - m)
            denom = jnp.sum(p, axis=-1, keepdims=True) + jnp.exp(sink - m)
            probs = (p / denom).astype(BF16)
            outs.append(jnp.dot(probs, v, preferred_element_type=F32))
        o_ref[:, grp * LANES:(grp + 1) * LANES] = jnp.where(low_half, outs[0], outs[1]).astype(BF16)


def _swa(qa, ka, va, sinks):
    cur = lambda b, n: (b * N_QBLOCKS + n, 0)
    prev = lambda b, n: (b * N_QBLOCKS + jnp.maximum(n - 1, 0), 0)
    return pl.pallas_call(
        _swa_kernel,
        grid=(BATCH, N_QBLOCKS),
        in_specs=[
            pl.BlockSpec(memory_space=pltpu.SMEM),
            pl.BlockSpec((BLOCK, D_SWA_Q), cur),
            pl.BlockSpec((BLOCK, D_SWA_KV_DUP), prev),
            pl.BlockSpec((BLOCK, D_SWA_KV_DUP), cur),
            pl.BlockSpec((BLOCK, D_SWA_KV_DUP), prev),
            pl.BlockSpec((BLOCK, D_SWA_KV_DUP), cur),
        ],
        out_specs=pl.BlockSpec((BLOCK, D_SWA_Q), cur),
        out_shape=jax.ShapeDtypeStruct((TOKENS, D_SWA_Q), BF16),
        compiler_params=_params(2),
        name="swa",
    )(sinks, qa, ka, ka, va, va)


def _sb_kernel(q_ref, k_ref, v_ref, tri_ref, o_ref, acc_ref, carry_ref):
    i = pl.program_id(2)
    lane = lax.broadcasted_iota(jnp.int32, (BLOCK, LANES), 1)
    low_half = lane < HEAD_DIM
    q = q_ref[...]
    zero = jnp.zeros((), BF16)
    q2 = jnp.concatenate([jnp.where(low_half, q, zero), jnp.where(low_half, zero, q)], axis=0)
    acc_ref[...] = jnp.zeros_like(acc_ref)
    carry_ref[...] = jnp.zeros_like(carry_ref)

    def tile(start, width, diagonal):
        k = k_ref[pl.ds(start, width), :]
        v = v_ref[pl.ds(start, width), :]
        z = lax.dot_general(q2, k, NT_DIMS, preferred_element_type=F32)
        softplus = jnp.maximum(z, 0.0) + jnp.log(1.0 + jnp.exp(-jnp.abs(z)))
        if diagonal:
            qpos = lax.broadcasted_iota(jnp.int32, (2 * BLOCK, width), 0) % BLOCK
            kpos = lax.broadcasted_iota(jnp.int32, (2 * BLOCK, width), 1)
            strict = kpos < qpos
            log_keep = jnp.where(strict, -softplus, 0.0)
        else:
            log_keep = -softplus
        hi = log_keep.astype(BF16)
        lo = (log_keep - hi.astype(F32)).astype(BF16)
        tri = tri_ref[:width, :width]
        suffix = (jnp.dot(hi, tri, preferred_element_type=F32)
                  + jnp.dot(lo, tri, preferred_element_type=F32))
        carry = carry_ref[...]
        carry_w = jnp.concatenate([carry] * (width // LANES), axis=1)
        a = jnp.exp(z + suffix + carry_w)
        if diagonal:
            a = jnp.where(strict, a, 0.0)
        acc_ref[...] += jnp.dot(a.astype(BF16), v, preferred_element_type=F32)
        carry_ref[...] = carry + jnp.broadcast_to(suffix[:, 0:1], (2 * BLOCK, LANES))

    tile(pl.multiple_of(i * BLOCK, BLOCK), BLOCK, True)

    def body(j, c):
        start = pl.multiple_of(i * BLOCK - (j + 1) * SB_KEY_TILE, BLOCK)
        tile(start, SB_KEY_TILE, False)
        return c

    lax.fori_loop(0, i // 2, body, 0)

    @pl.when(i % 2 == 1)
    def _():
        tile(0, BLOCK, False)

    o_ref[...] = jnp.where(low_half, acc_ref[:BLOCK, :], acc_ref[BLOCK:, :]).astype(BF16)


def _sb(qs, ks, vs, tri):
    n_groups = D_SB // LANES
    return pl.pallas_call(
        _sb_kernel,
        grid=(BATCH, n_groups, N_QBLOCKS),
        in_specs=[
            pl.BlockSpec((BLOCK, LANES), lambda b, g, i: (b * N_QBLOCKS + i, g)),
            pl.BlockSpec((SEQ, LANES), lambda b, g, i: (b, g)),
            pl.BlockSpec((SEQ, LANES), lambda b, g, i: (b, g)),
            pl.BlockSpec((SB_KEY_TILE, SB_KEY_TILE), lambda b, g, i: (0, 0)),
        ],
        out_specs=pl.BlockSpec((BLOCK, LANES), lambda b, g, i: (b * N_QBLOCKS + i, g)),
        out_shape=jax.ShapeDtypeStruct((TOKENS, D_SB), BF16),
        scratch_shapes=[pltpu.VMEM((2 * BLOCK, LANES), F32), pltpu.VMEM((2 * BLOCK, LANES), F32)],
        compiler_params=_params(3),
        name="stickbreak",
    )(qs, ks, vs, tri)


def _max4(a, b, c, d):
    return jnp.maximum(jnp.maximum(a, b), jnp.maximum(c, d))


def _first_index_of(vals, target):
    idx = jnp.full(target.shape, len(vals) - 1, jnp.int32)
    for p in range(len(vals) - 2, -1, -1):
        idx = jnp.where(vals[p] == target, p, idx)
    return idx


def _select_by_index(vals, idx):
    out = vals[-1]
    for p in range(len(vals) - 2, -1, -1):
        out = jnp.where(idx == p, vals[p], out)
    return out


def _mix_kernel(x_ref, ya_ref, yb_ref, wg_ref, bg_ref, wa_ref, wb_ref, wo_ref, lng_ref, lnb_ref,
                wrh_ref, wrl_ref, rb_ref, upper_ref,
                x1_ref, eid_ref, gate_ref, rank_ref, cnt_ref, base_ref):
    step = pl.program_id(0)
    x = x_ref[...]
    gates = jnp.dot(x.astype(BF16), wg_ref[...], preferred_element_type=F32) + bg_ref[...]
    pa = jnp.dot(ya_ref[...], wa_ref[...], preferred_element_type=F32)
    pb = jnp.dot(yb_ref[...], wb_ref[...], preferred_element_type=F32)
    merged = (jax.nn.sigmoid(gates[:, :D_MODEL]) * pa + jax.nn.sigmoid(gates[:, D_MODEL:]) * pb)
    mix = jnp.dot(merged.astype(BF16), wo_ref[...], preferred_element_type=F32)
    x1 = _layer_norm(DEEPNORM_ALPHA * x + mix, lng_ref[...], lnb_ref[...])
    x1_ref[...] = x1

    x1h = x1.astype(BF16)
    x1l = (x1 - x1h.astype(F32)).astype(BF16)
    wrh = wrh_ref[...]
    logits = (lax.dot_general(wrh, x1h, NT_DIMS, preferred_element_type=F32)
              + lax.dot_general(wrh, x1l, NT_DIMS, preferred_element_type=F32)
              + lax.dot_general(wrl_ref[...], x1h, NT_DIMS, preferred_element_type=F32))
    aff = jax.nn.sigmoid(logits)
    reps = ROW_BLOCK // LANES
    biased = aff + jnp.concatenate([rb_ref[...]] * reps, axis=1)
    bm = [biased[p * N_GROUPS:(p + 1) * N_GROUPS, :] for p in range(EXPERTS_PER_GROUP)]
    am = [aff[p * N_GROUPS:(p + 1) * N_GROUPS, :] for p in range(EXPERTS_PER_GROUP)]
    hi1, lo1 = jnp.maximum(bm[0], bm[1]), jnp.minimum(bm[0], bm[1])
    hi2, lo2 = jnp.maximum(bm[2], bm[3]), jnp.minimum(bm[2], bm[3])
    group_score = (jnp.maximum(hi1, hi2)
                   + jnp.maximum(jnp.minimum(hi1, hi2), jnp.maximum(lo1, lo2)))
    group_iota = lax.broadcasted_iota(jnp.int32, (N_GROUPS, ROW_BLOCK), 0)
    best = jnp.max(group_score, axis=0, keepdims=True)
    g_sel = jnp.min(jnp.where(group_score == best, group_iota, N_GROUPS), axis=0, keepdims=True)
    in_sel = group_iota == g_sel
    bsel = [jnp.sum(jnp.where(in_sel, b, 0.0), axis=0, keepdims=True) for b in bm]
    asel = [jnp.sum(jnp.where(in_sel, a, 0.0), axis=0, keepdims=True) for a in am]
    i1 = _first_index_of(bsel, _max4(*bsel))
    rest = [jnp.where(i1 == p, -jnp.inf, bsel[p]) for p in range(EXPERTS_PER_GROUP)]
    i2 = _first_index_of(rest, _max4(*rest))
    a1 = _select_by_index(asel, i1)
    a2 = _select_by_index(asel, i2)
    e1 = g_sel * EXPERTS_PER_GROUP + i1
    e2 = g_sel * EXPERTS_PER_GROUP + i2
    eid_ref[...] = jnp.concatenate([e1, e2], axis=0)
    gate_ref[...] = jnp.concatenate([a1 / (a1 + a2), a2 / (a1 + a2)], axis=0)

    @pl.when(step == 0)
    def _():
        base_ref[...] = jnp.zeros_like(base_ref)

    expert_iota = lax.broadcasted_iota(jnp.int32, (N_EXPERTS, ROW_BLOCK), 0)
    oh1 = expert_iota == e1
    oh2 = expert_iota == e2
    upper = upper_ref[...]
    c1 = jnp.dot(jnp.where(oh1, 1.0, 0.0).astype(BF16), upper, preferred_element_type=F32)
    c2 = jnp.dot(jnp.where(oh2, 1.0, 0.0).astype(BF16), upper, preferred_element_type=F32)
    base = base_ref[...]
    tot1 = c1[:, ROW_BLOCK:]
    tot2 = c2[:, ROW_BLOCK:]
    before1 = jnp.concatenate([base] * reps, axis=1) + c1[:, :ROW_BLOCK]
    before2 = jnp.concatenate([base + tot1] * reps, axis=1) + c2[:, :ROW_BLOCK]
    r1 = jnp.sum(jnp.where(oh1, before1, 0.0), axis=0, keepdims=True)
    r2 = jnp.sum(jnp.where(oh2, before2, 0.0), axis=0, keepdims=True)
    rank_ref[...] = jnp.concatenate([r1, r2], axis=0).astype(jnp.int32)
    new_base = base + tot1 + tot2
    base_ref[...] = new_base
    cnt_ref[...] = new_base


def _mix(x2, ya, yb, lw, shared):
    row = lambda i: (i, 0)
    fixed = lambda i: (0, 0)
    col = lambda i: (0, i)
    full = lambda a: pl.BlockSpec(a.shape, fixed)
    weights = [lw["w_g"], lw["b_g"], lw["w_a"], lw["w_b"], lw["w_o"], lw["ln1_g"], lw["ln1_b"],
               shared["wr_hi"], shared["wr_lo"], shared["rbias"], shared["upper"]]
    return pl.pallas_call(
        _mix_kernel,
        grid=(TOKENS // ROW_BLOCK,),
        in_specs=[pl.BlockSpec((ROW_BLOCK, D_MODEL), row),
                  pl.BlockSpec((ROW_BLOCK, D_SWA_Q), row),
                  pl.BlockSpec((ROW_BLOCK, D_SB), row)] + [full(w) for w in weights],
        out_specs=[pl.BlockSpec((ROW_BLOCK, D_MODEL), row),
                   pl.BlockSpec((TOP_K, ROW_BLOCK), col),
                   pl.BlockSpec((TOP_K, ROW_BLOCK), col),
                   pl.BlockSpec((TOP_K, ROW_BLOCK), col),
                   pl.BlockSpec((N_EXPERTS, LANES), fixed)],
        out_shape=[jax.ShapeDtypeStruct((TOKENS, D_MODEL), F32),
                   jax.ShapeDtypeStruct((TOP_K, TOKENS), jnp.int32),
                   jax.ShapeDtypeStruct((TOP_K, TOKENS), F32),
                   jax.ShapeDtypeStruct((TOP_K, TOKENS), jnp.int32),
                   jax.ShapeDtypeStruct((N_EXPERTS, LANES), F32)],
        scratch_shapes=[pltpu.VMEM((N_EXPERTS, LANES), F32)],
        compiler_params=_params(1),
        name="mix_ln_route",
    )(x2, ya, yb, *weights)


def _row_copy(src_hbm, src_row, dst_hbm, dst_row, sem):
    return pltpu.make_async_copy(src_hbm.at[pl.ds(src_row, 1)], dst_hbm.at[pl.ds(dst_row, 1)], sem)


def _dispatch_kernel(dest_ref, x1_hbm, xs_in_hbm, xs_hbm, sem):
    del xs_in_hbm
    step = pl.program_id(0)
    base = step * GATHER_ROWS

    def issue(t, c):
        for k in range(TOP_K):
            _row_copy(x1_hbm, base + t, xs_hbm, dest_ref[k, t], sem).start()
        return c

    lax.fori_loop(0, GATHER_ROWS, issue, 0, unroll=8)

    def drain(t, c):
        for k in range(TOP_K):
            _row_copy(x1_hbm, 0, xs_hbm, 0, sem).wait()
        return c

    lax.fori_loop(0, GATHER_ROWS, drain, 0, unroll=8)


def _dispatch(dest, x1, xs):
    return pl.pallas_call(
        _dispatch_kernel,
        grid=(TOKENS // GATHER_ROWS,),
        in_specs=[pl.BlockSpec((TOP_K, GATHER_ROWS), lambda i: (0, i), memory_space=pltpu.SMEM),
                  pl.BlockSpec(memory_space=pl.ANY),
                  pl.BlockSpec(memory_space=pl.ANY)],
        out_specs=pl.BlockSpec(memory_space=pl.ANY),
        out_shape=jax.ShapeDtypeStruct(xs.shape, xs.dtype),
        scratch_shapes=[pltpu.SemaphoreType.DMA(())],
        input_output_aliases={2: 0},
        compiler_params=_params(1),
        name="dispatch",
    )(dest, x1, xs)


def _expert_kernel(be_ref, nu_ref, x_ref, wg_ref, wu_ref, wd_ref, y_ref, wg_bf, wu_bf, wd_bf):
    i = pl.program_id(0)

    @pl.when(i < nu_ref[0])
    def _():
        e = be_ref[i]
        e_prev = be_ref[jnp.maximum(i - 1, 0)]

        @pl.when((i == 0) | (e != e_prev))
        def _():
            wg_bf[...] = wg_ref[...].astype(BF16)
            wu_bf[...] = wu_ref[...].astype(BF16)
            wd_bf[...] = wd_ref[...].astype(BF16)

        xb = x_ref[...].astype(BF16)
        hg = jnp.dot(xb, wg_bf[...], preferred_element_type=F32)
        hu = jnp.dot(xb, wu_bf[...], preferred_element_type=F32)
        h = hg * jax.nn.sigmoid(hg) * hu
        y_ref[...] = jnp.dot(h.astype(BF16), wd_bf[...], preferred_element_type=F32)

    @pl.when(i >= nu_ref[0])
    def _():
        y_ref[...] = jnp.zeros_like(y_ref)


def _experts(layer, block_expert, n_used, xs, w_gate, w_up, w_down):
    blk = lambda i, be, nu: (jnp.minimum(i, nu[0] - 1), 0)
    out_blk = lambda i, be, nu: (i, 0)
    wsel = lambda i, be, nu: (layer, be[jnp.minimum(i, nu[0] - 1)], 0, 0)
    grid_spec = pltpu.PrefetchScalarGridSpec(
        num_scalar_prefetch=2,
        grid=(N_EXPERT_BLOCKS,),
        in_specs=[pl.BlockSpec((EXPERT_ROWS, D_MODEL), blk),
                  pl.BlockSpec((None, None, D_MODEL, D_EXPERT), wsel),
                  pl.BlockSpec((None, None, D_MODEL, D_EXPERT), wsel),
                  pl.BlockSpec((None, None, D_EXPERT, D_MODEL), wsel)],
        out_specs=pl.BlockSpec((EXPERT_ROWS, D_MODEL), out_blk),
        scratch_shapes=[pltpu.VMEM((D_MODEL, D_EXPERT), BF16),
                        pltpu.VMEM((D_MODEL, D_EXPERT), BF16),
                        pltpu.VMEM((D_EXPERT, D_MODEL), BF16)],
    )
    return pl.pallas_call(
        _expert_kernel,
        grid_spec=grid_spec,
        out_shape=jax.ShapeDtypeStruct((N_EXPERT_BLOCKS * EXPERT_ROWS, D_MODEL), F32),
        compiler_params=_params(1),
        name="experts",
    )(block_expert, n_used, xs, w_gate, w_up, w_down)


def _combine_kernel(dest_ref, dest_next_ref, x1_ref, gate_ref, lng_ref, lnb_ref, ys_hbm,
                    x2_ref, buf_ref, sems):
    step = pl.program_id(0)
    n_steps = pl.num_programs(0)
    slot = step % 2

    def issue(d_ref, slot_):
        def body(t, c):
            for k in range(TOP_K):
                pltpu.make_async_copy(ys_hbm.at[pl.ds(d_ref[k, t], 1)],
                                      buf_ref.at[slot_, k, pl.ds(t, 1)], sems.at[slot_]).start()
            return c
        lax.fori_loop(0, GATHER_ROWS, body, 0, unroll=8)

    @pl.when(step == 0)
    def _():
        issue(dest_ref, 0)

    @pl.when(step + 1 < n_steps)
    def _():
        issue(dest_next_ref, 1 - slot)

    def drain(t, c):
        for k in range(TOP_K):
            pltpu.make_async_copy(ys_hbm.at[pl.ds(0, 1)], buf_ref.at[slot, k, pl.ds(0, 1)],
                                  sems.at[slot]).wait()
        return c

    lax.fori_loop(0, GATHER_ROWS, drain, 0, unroll=8)
    g = gate_ref[...]
    y = buf_ref[slot, 0] * g[:, 0:1] + buf_ref[slot, 1] * g[:, 1:2]
    x2_ref[...] = _layer_norm(DEEPNORM_ALPHA * x1_ref[...] + y, lng_ref[...], lnb_ref[...])


def _combine(dest, x1, gate_cols, ln_g, ln_b, ys):
    n_steps = TOKENS // GATHER_ROWS
    row = lambda i: (i, 0)
    fixed = lambda i: (0, 0)
    return pl.pallas_call(
        _combine_kernel,
        grid=(n_steps,),
        in_specs=[pl.BlockSpec((TOP_K, GATHER_ROWS), lambda i: (0, i), memory_space=pltpu.SMEM),
                  pl.BlockSpec((TOP_K, GATHER_ROWS), lambda i: (0, jnp.minimum(i + 1, n_steps - 1)),
                               memory_space=pltpu.SMEM),
                  pl.BlockSpec((GATHER_ROWS, D_MODEL), row),
                  pl.BlockSpec((GATHER_ROWS, TOP_K), row),
                  pl.BlockSpec((1, D_MODEL), fixed),
                  pl.BlockSpec((1, D_MODEL), fixed),
                  pl.BlockSpec(memory_space=pl.ANY)],
        out_specs=pl.BlockSpec((GATHER_ROWS, D_MODEL), row),
        out_shape=jax.ShapeDtypeStruct((TOKENS, D_MODEL), F32),
        scratch_shapes=[pltpu.VMEM((2, TOP_K, GATHER_ROWS, D_MODEL), F32),
                        pltpu.SemaphoreType.DMA((2,))],
        compiler_params=_params(1),
        name="combine_ln",
    )(dest, dest, x1, gate_cols, ln_g, ln_b, ys)


def _prepare_layer(l, w_in, b_in, w_branch_a, w_branch_b, w_out, ln1_g, ln1_b):
    def dup_heads(lo):
        h = jnp.arange(HEAD_DIM)
        return jnp.concatenate([lo + kv * HEAD_DIM + h for kv in range(SWA_KV_HEADS) for _ in range(2)])

    scale = HEAD_DIM ** -0.5
    q_a = jnp.arange(0, 512)
    k_a = dup_heads(512)
    v_a = dup_heads(640)
    q_s = jnp.arange(768, 1280)
    kv_s = jnp.arange(1280, 2304)
    cols = jnp.concatenate([q_a, k_a, v_a, q_s, kv_s])
    col_scale = jnp.concatenate([jnp.full((512,), scale), jnp.ones((512,)),
                                 jnp.full((512,), scale), jnp.ones((1024,))]).astype(F32)
    w, b = w_in[l], b_in[l]
    return {
        "w_qkv": (w[:, cols] * col_scale).astype(BF16),
        "b_qkv": (b[cols] * col_scale)[None, :],
        "w_g": w[:, 2304:].astype(BF16),
        "b_g": b[None, 2304:],
        "w_a": w_branch_a[l].astype(BF16),
        "w_b": w_branch_b[l].astype(BF16),
        "w_o": w_out[l].astype(BF16),
        "ln1_g": ln1_g[l][None, :],
        "ln1_b": ln1_b[l][None, :],
    }


def _prepare_shared(w_router, router_bias):
    r = jnp.arange(N_EXPERTS)
    perm = (r % N_GROUPS) * EXPERTS_PER_GROUP + r // N_GROUPS
    wr_t = w_router.T[perm].astype(F32)
    wr_hi = wr_t.astype(BF16)
    wr_lo = (wr_t - wr_hi.astype(F32)).astype(BF16)
    rbias = jnp.broadcast_to(router_bias.astype(F32)[perm][:, None], (N_EXPERTS, LANES))
    t = jnp.arange(ROW_BLOCK)
    strict_upper = (t[:, None] < t[None, :]).astype(BF16)
    upper = jnp.concatenate([strict_upper, jnp.ones((ROW_BLOCK, LANES), BF16)], axis=1)
    k = jnp.arange(SB_KEY_TILE)
    tri = (k[:, None] >= k[None, :]).astype(BF16)
    return {"wr_hi": wr_hi, "wr_lo": wr_lo, "rbias": rbias, "upper": upper, "tri": tri}


def _block_plan(counts, eid, rank):
    n_blocks = (counts + EXPERT_ROWS - 1) // EXPERT_ROWS
    block_end = jnp.cumsum(n_blocks)
    block_start = block_end - n_blocks
    n_used = block_end[-1:].astype(jnp.int32)
    block_expert = jnp.minimum(
        jnp.searchsorted(block_end, jnp.arange(N_EXPERT_BLOCKS), side="right"),
        N_EXPERTS - 1).astype(jnp.int32)
    dest = (block_start * EXPERT_ROWS)[eid] + rank
    return block_expert, n_used, dest.astype(jnp.int32)


def kernel(x, w_in, b_in, attn_sinks, w_branch_a, w_branch_b, w_out, ln1_g, ln1_b, w_router,
           router_bias, w_gate, w_up, w_down, ln2_g, ln2_b):
    shared = _prepare_shared(w_router, router_bias)
    x2 = x.reshape(TOKENS, D_MODEL)
    xs = jnp.zeros((N_EXPERT_BLOCKS * EXPERT_ROWS, D_MODEL), F32)
    for l in range(DEPTH):
        lw = _prepare_layer(l, w_in, b_in, w_branch_a, w_branch_b, w_out, ln1_g, ln1_b)
        qa, ka, va, qs, ks, vs = _inproj(x2, lw["w_qkv"], lw["b_qkv"])
        ya = _swa(qa, ka, va, attn_sinks[l])
        yb = _sb(qs, ks, vs, shared["tri"])
        x1, eid, gate, rank, cnt = _mix(x2, ya, yb, lw, shared)
        block_expert, n_used, dest = _block_plan(cnt[:, 0].astype(jnp.int32), eid, rank)
        xs = _dispatch(dest, x1, xs)
        ys = _experts(l, block_expert, n_used, xs, w_gate, w_up, w_down)
        x2 = _combine(dest, x1, gate.T, ln2_g[l][None, :], ln2_b[l][None, :], ys)
    return x2.reshape(BATCH, SEQ, D_MODEL)
```

```python
import jax
import jax.numpy as jnp
from jax import lax
from jax.experimental import pallas as pl
from jax.experimental.pallas import tpu as pltpu

D_MODEL = 1024
BATCH = 16
SEQ = 2048
DEPTH = 4
HEAD_DIM = 64
SWA_Q_HEADS = 8
SWA_KV_HEADS = 2
SWA_GROUP = SWA_Q_HEADS // SWA_KV_HEADS
SB_HEADS = 8
BLOCK = 128
N_EXPERTS = 32
N_GROUPS = 8
EXPERTS_PER_GROUP = N_EXPERTS // N_GROUPS
TOP_K = 2
D_EXPERT = 512
LN_EPS = 1e-5
DEEPNORM_ALPHA = (2 * DEPTH) ** 0.25

TOKENS = BATCH * SEQ
N_QBLOCKS = SEQ // BLOCK
LANES = 128
SUBLANES = 8
TILE_ROWS = D_MODEL // LANES
assert TILE_ROWS == SUBLANES
D_SWA_Q = SWA_Q_HEADS * HEAD_DIM
D_SWA_KV_DUP = 2 * SWA_KV_HEADS * HEAD_DIM
D_SB = SB_HEADS * HEAD_DIM
SB_LANE_GROUPS = D_SB // LANES
QKV_COLS = D_SWA_Q + 2 * D_SWA_KV_DUP + 3 * D_SB

ROW_BLOCK = 512
SB_KEY_TILE = 256
SB_PIPE_LAG = 2
EXPERT_ROWS = 512
N_EXPERT_BLOCKS = (TOKENS * TOP_K) // EXPERT_ROWS + N_EXPERTS
DISPATCH_ROWS = 256
GATHER_ROWS = 256
VMEM_LIMIT = 56 * 1024 * 1024
LOG2E = 1.4426950408889634

BF16 = jnp.bfloat16
F32 = jnp.float32
NT_DIMS = (((1,), (1,)), ((), ()))


def _params(n_axes, vmem=VMEM_LIMIT):
    return pltpu.CompilerParams(dimension_semantics=("arbitrary",) * n_axes,
                                vmem_limit_bytes=vmem)


def _layer_norm(h, g, b):
    mu = jnp.mean(h, axis=-1, keepdims=True)
    c = h - mu
    var = jnp.mean(c * c, axis=-1, keepdims=True)
    return c * lax.rsqrt(var + LN_EPS) * g + b


def _load_token_tiles(ref, n_tokens):
    return jnp.concatenate(
        [ref[pl.ds(j, n_tokens, stride=TILE_ROWS), :] for j in range(TILE_ROWS)], axis=1)


def _store_token_tiles(ref, value, n_tokens):
    for j in range(TILE_ROWS):
        ref[pl.ds(j, n_tokens, stride=TILE_ROWS), :] = value[:, j * LANES:(j + 1) * LANES]


_QKV_SLICES = ((0, 512), (512, 768), (768, 1024), (1024, 1536), (1536, 2048), (2048, 2560))


def _inproj_kernel(x_ref, w_ref, b_ref, *out_refs):
    xb = x_ref[...].astype(BF16)
    for (lo, hi), o_ref in zip(_QKV_SLICES, out_refs):
        p = jnp.dot(xb, w_ref[:, lo:hi], preferred_element_type=F32) + b_ref[:, lo:hi]
        o_ref[...] = p.astype(BF16)


def _inproj(x2, w_qkv, b_qkv):
    widths = [hi - lo for lo, hi in _QKV_SLICES]
    return pl.pallas_call(
        _inproj_kernel,
        grid=(TOKENS // ROW_BLOCK,),
        in_specs=[
            pl.BlockSpec((ROW_BLOCK, D_MODEL), lambda i: (i, 0)),
            pl.BlockSpec((D_MODEL, QKV_COLS), lambda i: (0, 0)),
            pl.BlockSpec((1, QKV_COLS), lambda i: (0, 0)),
        ],
        out_specs=[pl.BlockSpec((ROW_BLOCK, w), lambda i: (i, 0)) for w in widths],
        out_shape=[jax.ShapeDtypeStruct((TOKENS, w), BF16) for w in widths],
        compiler_params=_params(1),
        name="inproj",
    )(x2, w_qkv, b_qkv)


def _swa_kernel(sink_ref, q_ref, kp_ref, kc_ref, vp_ref, vc_ref, o_ref):
    n = pl.program_id(1)
    lane = lax.broadcasted_iota(jnp.int32, (BLOCK, LANES), 1)
    low_half = lane < HEAD_DIM
    qi = lax.broadcasted_iota(jnp.int32, (BLOCK, 2 * BLOCK), 0)
    kj = lax.broadcasted_iota(jnp.int32, (BLOCK, 2 * BLOCK), 1)
    dist = qi + BLOCK - kj
    valid = (dist >= 0) & (dist < BLOCK) & ((kj >= BLOCK) | (n > 0))
    dist_f = dist.astype(F32)
    zero = jnp.zeros((), BF16)
    def window(ref_prev, ref_cur, kvh):
        return jnp.concatenate([ref_prev[:, kvh * LANES:(kvh + 1) * LANES],
                                ref_cur[:, kvh * LANES:(kvh + 1) * LANES]], axis=0)

    scores = []
    for head in range(SWA_Q_HEADS):
        grp = head // 2
        q = q_ref[:, grp * LANES:(grp + 1) * LANES]
        qm = jnp.where(low_half, q, zero) if head % 2 == 0 else jnp.where(low_half, zero, q)
        scores.append(lax.dot_general(qm, window(kp_ref, kc_ref, head // SWA_GROUP), NT_DIMS,
                                      preferred_element_type=F32))
    outs = []
    for head in range(SWA_Q_HEADS):
        slope = 2.0 ** (-8.0 * (head + 1) / SWA_Q_HEADS)
        sink = sink_ref[head]
        s = jnp.where(valid, scores[head] - slope * dist_f, -jnp.inf)
        m = jnp.maximum(jnp.max(s, axis=-1, keepdims=True), sink)
        p = jnp.exp(s - m)
        denom = jnp.sum(p, axis=-1, keepdims=True) + jnp.exp(sink - m)
        pv = jnp.dot(p.astype(BF16), window(vp_ref, vc_ref, head // SWA_GROUP),
                     preferred_element_type=F32)
        outs.append(pv / denom)
    for grp in range(D_SWA_Q // LANES):
        o_ref[:, grp * LANES:(grp + 1) * LANES] = jnp.where(
            low_half, outs[2 * grp], outs[2 * grp + 1]).astype(BF16)


def _swa(qa, ka, va, sinks):
    cur = lambda b, n: (b * N_QBLOCKS + n, 0)
    prev = lambda b, n: (b * N_QBLOCKS + jnp.maximum(n - 1, 0), 0)
    return pl.pallas_call(
        _swa_kernel,
        grid=(BATCH, N_QBLOCKS),
        in_specs=[
            pl.BlockSpec(memory_space=pltpu.SMEM),
            pl.BlockSpec((BLOCK, D_SWA_Q), cur),
            pl.BlockSpec((BLOCK, D_SWA_KV_DUP), prev),
            pl.BlockSpec((BLOCK, D_SWA_KV_DUP), cur),
            pl.BlockSpec((BLOCK, D_SWA_KV_DUP), prev),
            pl.BlockSpec((BLOCK, D_SWA_KV_DUP), cur),
        ],
        out_specs=pl.BlockSpec((BLOCK, D_SWA_Q), cur),
        out_shape=jax.ShapeDtypeStruct((TOKENS, D_SWA_Q), BF16),
        compiler_params=_params(2),
        name="swa",
    )(sinks, qa, ka, ka, va, va)


def _sb_kernel(q_ref, k_ref, v_ref, ntri_ref, o_ref, qm_ref, acc_ref, carry_ref, z_ref):
    i = pl.program_id(1)
    lane = lax.broadcasted_iota(jnp.int32, (BLOCK, LANES), 1)
    low_half = lane < HEAD_DIM
    zero = jnp.zeros((), BF16)
    for h in range(SB_HEADS):
        q = q_ref[:, (h // 2) * LANES:(h // 2 + 1) * LANES]
        qm_ref[h] = jnp.where(low_half, q, zero) if h % 2 == 0 else jnp.where(low_half, zero, q)
    acc_ref[...] = jnp.zeros_like(acc_ref)
    carry_ref[...] = jnp.zeros_like(carry_ref)
    sign_bit = jnp.uint32(0x80000000)

    def scores(h, start, width):
        g = h // 2
        k = k_ref[pl.ds(start, width), g * LANES:(g + 1) * LANES]
        return lax.dot_general(qm_ref[h], k, NT_DIMS, preferred_element_type=F32)

    def sweep(load_z, start, width, tiles_done, strict=None):
        if tiles_done is not None:
            next_start = pl.multiple_of(
                jnp.maximum(i * BLOCK - (tiles_done + 1) * SB_KEY_TILE, 0), BLOCK)
        ntri = ntri_ref[:width, :width]
        suffixes = {}
        for step in range(SB_HEADS + SB_PIPE_LAG):
            if step < SB_HEADS:
                z = load_z(step)
                neg_abs = lax.bitcast_convert_type(
                    lax.bitcast_convert_type(z, jnp.uint32) | sign_bit, F32)
                sp = jnp.maximum(z, 0.0) + jnp.log(1.0 + jnp.exp2(neg_abs * LOG2E))
                if strict is not None:
                    sp = jnp.where(strict, sp, 0.0)
                suffixes[step] = jnp.dot(sp.astype(BF16), ntri, preferred_element_type=F32)
            h = step - SB_PIPE_LAG
            if h >= 0:
                suffix = suffixes.pop(h)
                carry = carry_ref[h]
                carry_w = jnp.concatenate([carry] * (width // LANES), axis=1)
                a = jnp.exp2((load_z(h) + suffix + carry_w) * LOG2E)
                if strict is not None:
                    a = jnp.where(strict, a, 0.0)
                g = h // 2
                v = v_ref[pl.ds(start, width), g * LANES:(g + 1) * LANES]
                acc_ref[h] += jnp.dot(a.astype(BF16), v, preferred_element_type=F32)
                carry_ref[h] = carry + jnp.broadcast_to(suffix[:, 0:1], (BLOCK, LANES))
                if tiles_done is not None:
                    z_ref[h] = scores(h, next_start, SB_KEY_TILE)

    n_full = i // 2
    qpos = lax.broadcasted_iota(jnp.int32, (BLOCK, BLOCK), 0)
    kpos = lax.broadcasted_iota(jnp.int32, (BLOCK, BLOCK), 1)
    diag_start = pl.multiple_of(i * BLOCK, BLOCK)
    diag_scores = [scores(h, diag_start, BLOCK) for h in range(SB_HEADS)]
    sweep(lambda h: diag_scores[h], diag_start, BLOCK, 0, kpos < qpos)

    def body(j, c):
        start = pl.multiple_of(i * BLOCK - (j + 1) * SB_KEY_TILE, BLOCK)
        sweep(lambda h: z_ref[h], start, SB_KEY_TILE, j + 1)
        return c

    lax.fori_loop(0, n_full, body, 0)

    @pl.when(i % 2 == 1)
    def _():
        sweep(lambda h: z_ref[h, :, :BLOCK], 0, BLOCK, None)


    for g in range(SB_LANE_GROUPS):
        o_ref[:, g * LANES:(g + 1) * LANES] = jnp.where(
            low_half, acc_ref[2 * g], acc_ref[2 * g + 1]).astype(BF16)


def _sb(qs, ks, vs, ntri):
    return pl.pallas_call(
        _sb_kernel,
        grid=(BATCH, N_QBLOCKS),
        in_specs=[
            pl.BlockSpec((BLOCK, D_SB), lambda b, i: (b * N_QBLOCKS + i, 0)),
            pl.BlockSpec((SEQ, D_SB), lambda b, i: (b, 0)),
            pl.BlockSpec((SEQ, D_SB), lambda b, i: (b, 0)),
            pl.BlockSpec((SB_KEY_TILE, SB_KEY_TILE), lambda b, i: (0, 0)),
        ],
        out_specs=pl.BlockSpec((BLOCK, D_SB), lambda b, i: (b * N_QBLOCKS + i, 0)),
        out_shape=jax.ShapeDtypeStruct((TOKENS, D_SB), BF16),
        scratch_shapes=[pltpu.VMEM((SB_HEADS, BLOCK, LANES), BF16),
                        pltpu.VMEM((SB_HEADS, BLOCK, LANES), F32),
                        pltpu.VMEM((SB_HEADS, BLOCK, LANES), F32),
                        pltpu.VMEM((SB_HEADS, BLOCK, SB_KEY_TILE), F32)],
        compiler_params=_params(2),
        name="stickbreak",
    )(qs, ks, vs, ntri)


def _max4(a, b, c, d):
    return jnp.maximum(jnp.maximum(a, b), jnp.maximum(c, d))


def _first_index_of(vals, target):
    idx = jnp.full(target.shape, len(vals) - 1, jnp.int32)
    for p in range(len(vals) - 2, -1, -1):
        idx = jnp.where(vals[p] == target, p, idx)
    return idx


def _select_by_index(vals, idx):
    out = vals[-1]
    for p in range(len(vals) - 2, -1, -1):
        out = jnp.where(idx == p, vals[p], out)
    return out


def _mix_kernel(x_ref, ya_ref, yb_ref, wg_ref, bg_ref, wa_ref, wb_ref, wo_ref, lng_ref, lnb_ref,
                wrh_ref, wrl_ref, rb_ref, upper_ref,
                x1_ref, eid_ref, gate_ref, rank_ref, cnt_ref, base_ref):
    step = pl.program_id(0)
    x = x_ref[...]
    gates = jnp.dot(x.astype(BF16), wg_ref[...], preferred_element_type=F32) + bg_ref[...]
    pa = jnp.dot(ya_ref[...], wa_ref[...], preferred_element_type=F32)
    pb = jnp.dot(yb_ref[...], wb_ref[...], preferred_element_type=F32)
    merged = (jax.nn.sigmoid(gates[:, :D_MODEL]) * pa + jax.nn.sigmoid(gates[:, D_MODEL:]) * pb)
    mix = jnp.dot(merged.astype(BF16), wo_ref[...], preferred_element_type=F32)
    x1 = _layer_norm(DEEPNORM_ALPHA * x + mix, lng_ref[...], lnb_ref[...])
    _store_token_tiles(x1_ref, x1, ROW_BLOCK)

    x1h = x1.astype(BF16)
    x1l = (x1 - x1h.astype(F32)).astype(BF16)
    wrh = wrh_ref[...]
    logits = (lax.dot_general(wrh, x1h, NT_DIMS, preferred_element_type=F32)
              + lax.dot_general(wrh, x1l, NT_DIMS, preferred_element_type=F32)
              + lax.dot_general(wrl_ref[...], x1h, NT_DIMS, preferred_element_type=F32))
    aff = jax.nn.sigmoid(logits)
    reps = ROW_BLOCK // LANES
    biased = aff + jnp.concatenate([rb_ref[...]] * reps, axis=1)
    bm = [biased[p * N_GROUPS:(p + 1) * N_GROUPS, :] for p in range(EXPERTS_PER_GROUP)]
    am = [aff[p * N_GROUPS:(p + 1) * N_GROUPS, :] for p in range(EXPERTS_PER_GROUP)]
    hi1, lo1 = jnp.maximum(bm[0], bm[1]), jnp.minimum(bm[0], bm[1])
    hi2, lo2 = jnp.maximum(bm[2], bm[3]), jnp.minimum(bm[2], bm[3])
    group_score = (jnp.maximum(hi1, hi2)
                   + jnp.maximum(jnp.minimum(hi1, hi2), jnp.maximum(lo1, lo2)))
    group_iota = lax.broadcasted_iota(jnp.int32, (N_GROUPS, ROW_BLOCK), 0)
    best = jnp.max(group_score, axis=0, keepdims=True)
    g_sel = jnp.min(jnp.where(group_score == best, group_iota, N_GROUPS), axis=0, keepdims=True)
    in_sel = group_iota == g_sel
    bsel = [jnp.sum(jnp.where(in_sel, b, 0.0), axis=0, keepdims=True) for b in bm]
    asel = [jnp.sum(jnp.where(in_sel, a, 0.0), axis=0, keepdims=True) for a in am]
    i1 = _first_index_of(bsel, _max4(*bsel))
    rest = [jnp.where(i1 == p, -jnp.inf, bsel[p]) for p in range(EXPERTS_PER_GROUP)]
    i2 = _first_index_of(rest, _max4(*rest))
    a1 = _select_by_index(asel, i1)
    a2 = _select_by_index(asel, i2)
    e1 = g_sel * EXPERTS_PER_GROUP + i1
    e2 = g_sel * EXPERTS_PER_GROUP + i2
    eid_ref[...] = jnp.concatenate([e1, e2], axis=0)
    gate_ref[...] = jnp.concatenate([a1 / (a1 + a2), a2 / (a1 + a2)], axis=0)

    @pl.when(step == 0)
    def _():
        base_ref[...] = jnp.zeros_like(base_ref)

    expert_iota = lax.broadcasted_iota(jnp.int32, (N_EXPERTS, ROW_BLOCK), 0)
    oh1 = expert_iota == e1
    oh2 = expert_iota == e2
    upper = upper_ref[...]
    c1 = jnp.dot(jnp.where(oh1, 1.0, 0.0).astype(BF16), upper, preferred_element_type=F32)
    c2 = jnp.dot(jnp.where(oh2, 1.0, 0.0).astype(BF16), upper, preferred_element_type=F32)
    base = base_ref[...]
    tot1 = c1[:, ROW_BLOCK:]
    tot2 = c2[:, ROW_BLOCK:]
    before1 = jnp.concatenate([base] * reps, axis=1) + c1[:, :ROW_BLOCK]
    before2 = jnp.concatenate([base + tot1] * reps, axis=1) + c2[:, :ROW_BLOCK]
    r1 = jnp.sum(jnp.where(oh1, before1, 0.0), axis=0, keepdims=True)
    r2 = jnp.sum(jnp.where(oh2, before2, 0.0), axis=0, keepdims=True)
    rank_ref[...] = jnp.concatenate([r1, r2], axis=0).astype(jnp.int32)
    new_base = base + tot1 + tot2
    base_ref[...] = new_base
    cnt_ref[...] = new_base


def _mix(x2, ya, yb, lw, shared):
    row = lambda i: (i, 0)
    fixed = lambda i: (0, 0)
    col = lambda i: (0, i)
    full = lambda a: pl.BlockSpec(a.shape, fixed)
    weights = [lw["w_g"], lw["b_g"], lw["w_a"], lw["w_b"], lw["w_o"], lw["ln1_g"], lw["ln1_b"],
               shared["wr_hi"], shared["wr_lo"], shared["rbias"], shared["upper"]]
    return pl.pallas_call(
        _mix_kernel,
        grid=(TOKENS // ROW_BLOCK,),
        in_specs=[pl.BlockSpec((ROW_BLOCK, D_MODEL), row),
                  pl.BlockSpec((ROW_BLOCK, D_SWA_Q), row),
                  pl.BlockSpec((ROW_BLOCK, D_SB), row)] + [full(w) for w in weights],
        out_specs=[pl.BlockSpec((ROW_BLOCK * TILE_ROWS, LANES), row),
                   pl.BlockSpec((TOP_K, ROW_BLOCK), col),
                   pl.BlockSpec((TOP_K, ROW_BLOCK), col),
                   pl.BlockSpec((TOP_K, ROW_BLOCK), col),
                   pl.BlockSpec((N_EXPERTS, LANES), fixed)],
        out_shape=[jax.ShapeDtypeStruct((TOKENS * TILE_ROWS, LANES), F32),
                   jax.ShapeDtypeStruct((TOP_K, TOKENS), jnp.int32),
                   jax.ShapeDtypeStruct((TOP_K, TOKENS), F32),
                   jax.ShapeDtypeStruct((TOP_K, TOKENS), jnp.int32),
                   jax.ShapeDtypeStruct((N_EXPERTS, LANES), F32)],
        scratch_shapes=[pltpu.VMEM((N_EXPERTS, LANES), F32)],
        compiler_params=_params(1),
        name="mix_ln_route",
    )(x2, ya, yb, *weights)


def _token_tile(ref, row):
    return ref.at[pl.ds(pl.multiple_of(row * TILE_ROWS, TILE_ROWS), TILE_ROWS)]


def _dispatch_kernel(dest_ref, x1_ref, xs_in_hbm, xs_hbm, sem):
    del xs_in_hbm

    def issue(t, c):
        for k in range(TOP_K):
            pltpu.make_async_copy(_token_tile(x1_ref, t),
                                  _token_tile(xs_hbm, dest_ref[TOP_K * t + k]), sem).start()
        return c

    lax.fori_loop(0, DISPATCH_ROWS, issue, 0, unroll=8)

    def drain(t, c):
        for k in range(TOP_K):
            pltpu.make_async_copy(_token_tile(x1_ref, 0), _token_tile(xs_hbm, 0), sem).wait()
        return c

    lax.fori_loop(0, DISPATCH_ROWS, drain, 0, unroll=8)


def _dispatch(dest, x1, xs):
    return pl.pallas_call(
        _dispatch_kernel,
        grid=(TOKENS // DISPATCH_ROWS,),
        in_specs=[pl.BlockSpec((TOP_K * DISPATCH_ROWS,), lambda i: (i,), memory_space=pltpu.SMEM),
                  pl.BlockSpec((DISPATCH_ROWS * TILE_ROWS, LANES), lambda i: (i, 0)),
                  pl.BlockSpec(memory_space=pl.ANY)],
        out_specs=pl.BlockSpec(memory_space=pl.ANY),
        out_shape=jax.ShapeDtypeStruct(xs.shape, xs.dtype),
        scratch_shapes=[pltpu.SemaphoreType.DMA(())],
        input_output_aliases={2: 0},
        compiler_params=_params(1),
        name="dispatch",
    )(dest, x1, xs)


def _expert_kernel(be_ref, nu_ref, x_ref, wg_ref, wu_ref, wd_ref, y_ref, wg_bf, wu_bf, wd_bf):
    i = pl.program_id(0)

    @pl.when(i < nu_ref[0])
    def _():
        e = be_ref[i]
        e_prev = be_ref[jnp.maximum(i - 1, 0)]

        @pl.when((i == 0) | (e != e_prev))
        def _():
            wg_bf[...] = wg_ref[...].astype(BF16)
            wu_bf[...] = wu_ref[...].astype(BF16)
            wd_bf[...] = wd_ref[...].astype(BF16)

        xb = _load_token_tiles(x_ref, EXPERT_ROWS).astype(BF16)
        hg = jnp.dot(xb, wg_bf[...], preferred_element_type=F32)
        hu = jnp.dot(xb, wu_bf[...], preferred_element_type=F32)
        h = hg * jax.nn.sigmoid(hg) * hu
        y = jnp.dot(h.astype(BF16), wd_bf[...], preferred_element_type=F32)
        _store_token_tiles(y_ref, y, EXPERT_ROWS)

    @pl.when(i >= nu_ref[0])
    def _():
        y_ref[...] = jnp.zeros_like(y_ref)


def _experts(layer, block_expert, n_used, xs, w_gate, w_up, w_down):
    blk = lambda i, be, nu: (jnp.minimum(i, nu[0] - 1), 0)
    out_blk = lambda i, be, nu: (i, 0)
    wsel = lambda i, be, nu: (layer, be[jnp.minimum(i, nu[0] - 1)], 0, 0)
    grid_spec = pltpu.PrefetchScalarGridSpec(
        num_scalar_prefetch=2,
        grid=(N_EXPERT_BLOCKS,),
        in_specs=[pl.BlockSpec((EXPERT_ROWS * TILE_ROWS, LANES), blk),
                  pl.BlockSpec((None, None, D_MODEL, D_EXPERT), wsel),
                  pl.BlockSpec((None, None, D_MODEL, D_EXPERT), wsel),
                  pl.BlockSpec((None, None, D_EXPERT, D_MODEL), wsel)],
        out_specs=pl.BlockSpec((EXPERT_ROWS * TILE_ROWS, LANES), out_blk),
        scratch_shapes=[pltpu.VMEM((D_MODEL, D_EXPERT), BF16),
                        pltpu.VMEM((D_MODEL, D_EXPERT), BF16),
                        pltpu.VMEM((D_EXPERT, D_MODEL), BF16)],
    )
    return pl.pallas_call(
        _expert_kernel,
        grid_spec=grid_spec,
        out_shape=jax.ShapeDtypeStruct((N_EXPERT_BLOCKS * EXPERT_ROWS * TILE_ROWS, LANES), F32),
        compiler_params=_params(1),
        name="experts",
    )(block_expert, n_used, xs, w_gate, w_up, w_down)


def _combine_kernel(dest_ref, dest_next_ref, x1_ref, gate_ref, lng_ref, lnb_ref, ys_hbm,
                    x2_ref, buf_ref, sems):
    step = pl.program_id(0)
    n_steps = pl.num_programs(0)
    slot = step % 2

    def issue(d_ref, slot_):
        def body(t, c):
            for k in range(TOP_K):
                pltpu.make_async_copy(_token_tile(ys_hbm, d_ref[TOP_K * t + k]),
                                      _token_tile(buf_ref.at[slot_, k], t), sems.at[slot_]).start()
            return c
        lax.fori_loop(0, GATHER_ROWS, body, 0, unroll=8)

    @pl.when(step == 0)
    def _():
        issue(dest_ref, 0)

    @pl.when(step + 1 < n_steps)
    def _():
        issue(dest_next_ref, 1 - slot)

    def drain(t, c):
        for k in range(TOP_K):
            pltpu.make_async_copy(_token_tile(ys_hbm, 0), _token_tile(buf_ref.at[slot, k], 0),
                                  sems.at[slot]).wait()
        return c

    lax.fori_loop(0, GATHER_ROWS, drain, 0, unroll=8)
    g = gate_ref[...]
    y = (_load_token_tiles(buf_ref.at[slot, 0], GATHER_ROWS) * g[:, 0:1]
         + _load_token_tiles(buf_ref.at[slot, 1], GATHER_ROWS) * g[:, 1:2])
    x1 = _load_token_tiles(x1_ref, GATHER_ROWS)
    x2_ref[...] = _layer_norm(DEEPNORM_ALPHA * x1 + y, lng_ref[...], lnb_ref[...])


def _combine(dest, x1, gate_cols, ln_g, ln_b, ys):
    n_steps = TOKENS // GATHER_ROWS
    row = lambda i: (i, 0)
    fixed = lambda i: (0, 0)
    return pl.pallas_call(
        _combine_kernel,
        grid=(n_steps,),
        in_specs=[pl.BlockSpec((TOP_K * GATHER_ROWS,), lambda i: (i,), memory_space=pltpu.SMEM),
                  pl.BlockSpec((TOP_K * GATHER_ROWS,), lambda i: (jnp.minimum(i + 1, n_steps - 1),),
                               memory_space=pltpu.SMEM),
                  pl.BlockSpec((GATHER_ROWS * TILE_ROWS, LANES), row),
                  pl.BlockSpec((GATHER_ROWS, TOP_K), row),
                  pl.BlockSpec((1, D_MODEL), fixed),
                  pl.BlockSpec((1, D_MODEL), fixed),
                  pl.BlockSpec(memory_space=pl.ANY)],
        out_specs=pl.BlockSpec((GATHER_ROWS, D_MODEL), row),
        out_shape=jax.ShapeDtypeStruct((TOKENS, D_MODEL), F32),
        scratch_shapes=[pltpu.VMEM((2, TOP_K, GATHER_ROWS * TILE_ROWS, LANES), F32),
                        pltpu.SemaphoreType.DMA((2,))],
        compiler_params=_params(1),
        name="combine_ln",
    )(dest, dest, x1, gate_cols, ln_g, ln_b, ys)


def _prepare_layer(l, w_in, b_in, w_branch_a, w_branch_b, w_out, ln1_g, ln1_b):
    def dup_heads(lo):
        h = jnp.arange(HEAD_DIM)
        return jnp.concatenate([lo + kv * HEAD_DIM + h for kv in range(SWA_KV_HEADS) for _ in range(2)])

    scale = HEAD_DIM ** -0.5
    q_a = jnp.arange(0, 512)
    k_a = dup_heads(512)
    v_a = dup_heads(640)
    q_s = jnp.arange(768, 1280)
    kv_s = jnp.arange(1280, 2304)
    cols = jnp.concatenate([q_a, k_a, v_a, q_s, kv_s])
    col_scale = jnp.concatenate([jnp.full((512,), scale), jnp.ones((512,)),
                                 jnp.full((512,), scale), jnp.ones((1024,))]).astype(F32)
    w, b = w_in[l], b_in[l]
    return {
        "w_qkv": (w[:, cols] * col_scale).astype(BF16),
        "b_qkv": (b[cols] * col_scale)[None, :],
        "w_g": w[:, 2304:].astype(BF16),
        "b_g": b[None, 2304:],
        "w_a": w_branch_a[l].astype(BF16),
        "w_b": w_branch_b[l].astype(BF16),
        "w_o": w_out[l].astype(BF16),
        "ln1_g": ln1_g[l][None, :],
        "ln1_b": ln1_b[l][None, :],
    }


def _prepare_shared(w_router, router_bias):
    r = jnp.arange(N_EXPERTS)
    perm = (r % N_GROUPS) * EXPERTS_PER_GROUP + r // N_GROUPS
    wr_t = w_router.T[perm].astype(F32)
    wr_hi = wr_t.astype(BF16)
    wr_lo = (wr_t - wr_hi.astype(F32)).astype(BF16)
    rbias = jnp.broadcast_to(router_bias.astype(F32)[perm][:, None], (N_EXPERTS, LANES))
    t = jnp.arange(ROW_BLOCK)
    strict_upper = (t[:, None] < t[None, :]).astype(BF16)
    upper = jnp.concatenate([strict_upper, jnp.ones((ROW_BLOCK, LANES), BF16)], axis=1)
    k = jnp.arange(SB_KEY_TILE)
    ntri = -(k[:, None] >= k[None, :]).astype(BF16)
    return {"wr_hi": wr_hi, "wr_lo": wr_lo, "rbias": rbias, "upper": upper, "ntri": ntri}


def _block_plan(counts, eid, rank):
    n_blocks = (counts + EXPERT_ROWS - 1) // EXPERT_ROWS
    block_end = jnp.cumsum(n_blocks)
    row_start = (block_end - n_blocks) * EXPERT_ROWS
    n_used = block_end[-1:].astype(jnp.int32)
    blocks = jnp.arange(N_EXPERT_BLOCKS)
    block_expert = jnp.minimum(jnp.sum(block_end[None, :] <= blocks[:, None], axis=1),
                               N_EXPERTS - 1).astype(jnp.int32)
    experts = jnp.arange(N_EXPERTS)
    start_of = jnp.sum(jnp.where(eid[:, :, None] == experts, row_start, 0), axis=-1)
    dest = (start_of + rank).astype(jnp.int32)
    return block_expert, n_used, dest.T.reshape(-1)


def kernel(x, w_in, b_in, attn_sinks, w_branch_a, w_branch_b, w_out, ln1_g, ln1_b, w_router,
           router_bias, w_gate, w_up, w_down, ln2_g, ln2_b):
    shared = _prepare_shared(w_router, router_bias)
    x2 = x.reshape(TOKENS, D_MODEL)
    xs = jnp.zeros((N_EXPERT_BLOCKS * EXPERT_ROWS * TILE_ROWS, LANES), F32)
    for l in range(DEPTH):
        lw = _prepare_layer(l, w_in, b_in, w_branch_a, w_branch_b, w_out, ln1_g, ln1_b)
        qa, ka, va, qs, ks, vs = _inproj(x2, lw["w_qkv"], lw["b_qkv"])
        ya = _swa(qa, ka, va, attn_sinks[l])
        yb = _sb(qs, ks, vs, shared["ntri"])
        x1, eid, gate, rank, cnt = _mix(x2, ya, yb, lw, shared)
        block_expert, n_used, dest = _block_plan(cnt[:, 0].astype(jnp.int32), eid, rank)
        xs = _dispatch(dest, x1, xs)
        ys = _experts(l, block_expert, n_used, xs, w_gate, w_up, w_down)
        x2 = _combine(dest, x1, gate.T, ln2_g[l][None, :], ln2_b[l][None, :], ys)
    return x2.reshape(BATCH, SEQ, D_MODEL)
```

```python
import jax
import jax.numpy as jnp
from jax import lax
from jax.experimental import pallas as pl
from jax.experimental.pallas import tpu as pltpu

D_MODEL = 1024
BATCH = 16
SEQ = 2048
DEPTH = 4
HEAD_DIM = 64
SWA_Q_HEADS = 8
SWA_KV_HEADS = 2
SWA_GROUP = SWA_Q_HEADS // SWA_KV_HEADS
SB_HEADS = 8
BLOCK = 128
N_EXPERTS = 32
N_GROUPS = 8
EXPERTS_PER_GROUP = N_EXPERTS // N_GROUPS
TOP_K = 2
D_EXPERT = 512
LN_EPS = 1e-5
DEEPNORM_ALPHA = (2 * DEPTH) ** 0.25

TOKENS = BATCH * SEQ
N_QBLOCKS = SEQ // BLOCK
LANES = 128
SUBLANES = 8
TILE_ROWS = D_MODEL // LANES
assert TILE_ROWS == SUBLANES
D_SWA_Q = SWA_Q_HEADS * HEAD_DIM
D_SWA_KV_DUP = 2 * SWA_KV_HEADS * HEAD_DIM
D_SB = SB_HEADS * HEAD_DIM
SB_LANE_GROUPS = D_SB // LANES
QKV_COLS = D_SWA_Q + 2 * D_SWA_KV_DUP + 3 * D_SB

ROW_BLOCK = 512
SB_KEY_TILE = 256
SB_PIPE_LAG = 2
SB_DEAD_LOG = -110.0
EXPERT_ROWS = 512
N_EXPERT_BLOCKS = (TOKENS * TOP_K) // EXPERT_ROWS + N_EXPERTS
DISPATCH_ROWS = 256
GATHER_ROWS = 256
VMEM_LIMIT = 56 * 1024 * 1024
LOG2E = 1.4426950408889634

BF16 = jnp.bfloat16
F32 = jnp.float32
NT_DIMS = (((1,), (1,)), ((), ()))


def _params(n_axes, vmem=VMEM_LIMIT):
    return pltpu.CompilerParams(dimension_semantics=("arbitrary",) * n_axes,
                                vmem_limit_bytes=vmem)


def _layer_norm(h, g, b):
    mu = jnp.mean(h, axis=-1, keepdims=True)
    c = h - mu
    var = jnp.mean(c * c, axis=-1, keepdims=True)
    return c * lax.rsqrt(var + LN_EPS) * g + b


def _load_token_tiles(ref, n_tokens):
    return jnp.concatenate(
        [ref[pl.ds(j, n_tokens, stride=TILE_ROWS), :] for j in range(TILE_ROWS)], axis=1)


def _store_token_tiles(ref, value, n_tokens):
    for j in range(TILE_ROWS):
        ref[pl.ds(j, n_tokens, stride=TILE_ROWS), :] = value[:, j * LANES:(j + 1) * LANES]


_QKV_SLICES = ((0, 512), (512, 768), (768, 1024), (1024, 1536), (1536, 2048), (2048, 2560))


def _inproj_kernel(x_ref, w_ref, b_ref, *out_refs):
    xb = x_ref[...].astype(BF16)
    for (lo, hi), o_ref in zip(_QKV_SLICES, out_refs):
        p = jnp.dot(xb, w_ref[:, lo:hi], preferred_element_type=F32) + b_ref[:, lo:hi]
        o_ref[...] = p.astype(BF16)


def _inproj(x2, w_qkv, b_qkv):
    widths = [hi - lo for lo, hi in _QKV_SLICES]
    return pl.pallas_call(
        _inproj_kernel,
        grid=(TOKENS // ROW_BLOCK,),
        in_specs=[
            pl.BlockSpec((ROW_BLOCK, D_MODEL), lambda i: (i, 0)),
            pl.BlockSpec((D_MODEL, QKV_COLS), lambda i: (0, 0)),
            pl.BlockSpec((1, QKV_COLS), lambda i: (0, 0)),
        ],
        out_specs=[pl.BlockSpec((ROW_BLOCK, w), lambda i: (i, 0)) for w in widths],
        out_shape=[jax.ShapeDtypeStruct((TOKENS, w), BF16) for w in widths],
        compiler_params=_params(1),
        name="inproj",
    )(x2, w_qkv, b_qkv)


def _swa_block(n, low_half, sink_ref, q_ref, kp_ref, kc_ref, vp_ref, vc_ref, o_ref):
    qi = lax.broadcasted_iota(jnp.int32, (BLOCK, 2 * BLOCK), 0)
    kj = lax.broadcasted_iota(jnp.int32, (BLOCK, 2 * BLOCK), 1)
    dist = qi + BLOCK - kj
    valid = (dist >= 0) & (dist < BLOCK) & ((kj >= BLOCK) | (n > 0))
    dist_f = dist.astype(F32)
    zero = jnp.zeros((), BF16)

    def window(ref_prev, ref_cur, kvh):
        return jnp.concatenate([ref_prev[:, kvh * LANES:(kvh + 1) * LANES],
                                ref_cur[:, kvh * LANES:(kvh + 1) * LANES]], axis=0)

    scores = []
    for head in range(SWA_Q_HEADS):
        grp = head // 2
        q = q_ref[:, grp * LANES:(grp + 1) * LANES]
        qm = jnp.where(low_half, q, zero) if head % 2 == 0 else jnp.where(low_half, zero, q)
        scores.append(lax.dot_general(qm, window(kp_ref, kc_ref, head // SWA_GROUP), NT_DIMS,
                                      preferred_element_type=F32))
    outs = []
    for head in range(SWA_Q_HEADS):
        slope = 2.0 ** (-8.0 * (head + 1) / SWA_Q_HEADS)
        sink = sink_ref[head]
        s = jnp.where(valid, scores[head] - slope * dist_f, -jnp.inf)
        m = jnp.maximum(jnp.max(s, axis=-1, keepdims=True), sink)
        p = jnp.exp(s - m)
        denom = jnp.sum(p, axis=-1, keepdims=True) + jnp.exp(sink - m)
        pv = jnp.dot(p.astype(BF16), window(vp_ref, vc_ref, head // SWA_GROUP),
                     preferred_element_type=F32)
        outs.append(pv / denom)
    for grp in range(D_SWA_Q // LANES):
        o_ref[:, grp * LANES:(grp + 1) * LANES] = jnp.where(
            low_half, outs[2 * grp], outs[2 * grp + 1]).astype(BF16)


def _sb_block(i, low_half, q_ref, k_ref, v_ref, ntri_ref, o_ref, qm_ref, acc_ref, carry_ref, z_ref):
    zero = jnp.zeros((), BF16)
    for h in range(SB_HEADS):
        q = q_ref[:, (h // 2) * LANES:(h // 2 + 1) * LANES]
        qm_ref[h] = jnp.where(low_half, q, zero) if h % 2 == 0 else jnp.where(low_half, zero, q)
    acc_ref[...] = jnp.zeros_like(acc_ref)
    carry_ref[...] = jnp.zeros_like(carry_ref)
    sign_bit = jnp.uint32(0x80000000)
    odd = i % 2
    full_end = (i - odd) * BLOCK
    n_full = i // 2

    def scores(h, start, width):
        g = h // 2
        k = k_ref[pl.ds(start, width), g * LANES:(g + 1) * LANES]
        return lax.dot_general(qm_ref[h], k, NT_DIMS, preferred_element_type=F32)

    def sweep(load_z, start, width, tiles_done, strict=None):
        next_start = pl.multiple_of(
            jnp.maximum(full_end - (tiles_done + 1) * SB_KEY_TILE, 0), BLOCK)
        ntri = ntri_ref[:width, :width]
        suffixes = {}
        for step in range(SB_HEADS + SB_PIPE_LAG):
            if step < SB_HEADS:
                z = load_z(step)
                neg_abs = lax.bitcast_convert_type(
                    lax.bitcast_convert_type(z, jnp.uint32) | sign_bit, F32)
                sp = jnp.maximum(z, 0.0) + jnp.log(1.0 + jnp.exp2(neg_abs * LOG2E))
                if strict is not None:
                    sp = jnp.where(strict, sp, 0.0)
                suffixes[step] = jnp.dot(sp.astype(BF16), ntri, preferred_element_type=F32)
            h = step - SB_PIPE_LAG
            if h >= 0:
                suffix = suffixes.pop(h)
                carry = carry_ref[h]
                carry_w = jnp.concatenate([carry] * (width // LANES), axis=1)
                a = jnp.exp2((load_z(h) + suffix + carry_w) * LOG2E)
                if strict is not None:
                    a = jnp.where(strict, a, 0.0)
                g = h // 2
                v = v_ref[pl.ds(start, width), g * LANES:(g + 1) * LANES]
                acc_ref[h] += jnp.dot(a.astype(BF16), v, preferred_element_type=F32)
                carry_ref[h] = carry + jnp.broadcast_to(suffix[:, 0:1], (BLOCK, LANES))
                z_ref[h] = scores(h, next_start, SB_KEY_TILE)

    @pl.when(odd == 0)
    def _():
        start = pl.multiple_of(i * BLOCK, BLOCK)
        qpos = lax.broadcasted_iota(jnp.int32, (BLOCK, BLOCK), 0)
        kpos = lax.broadcasted_iota(jnp.int32, (BLOCK, BLOCK), 1)
        diag = [scores(h, start, BLOCK) for h in range(SB_HEADS)]
        sweep(lambda h: diag[h], start, BLOCK, 0, kpos < qpos)

    @pl.when(odd == 1)
    def _():
        start = pl.multiple_of((i - 1) * BLOCK, BLOCK)
        qpos = lax.broadcasted_iota(jnp.int32, (BLOCK, SB_KEY_TILE), 0) + BLOCK
        kpos = lax.broadcasted_iota(jnp.int32, (BLOCK, SB_KEY_TILE), 1)
        for h in range(SB_HEADS):
            z_ref[h] = scores(h, start, SB_KEY_TILE)
        sweep(lambda h: z_ref[h], start, SB_KEY_TILE, 0, kpos < qpos)

    def body(state):
        j, _ = state
        start = pl.multiple_of(full_end - (j + 1) * SB_KEY_TILE, BLOCK)
        sweep(lambda h: z_ref[h], start, SB_KEY_TILE, j + 1)
        return j + 1, (jnp.max(carry_ref[...]) > SB_DEAD_LOG).astype(jnp.int32)

    lax.while_loop(lambda state: (state[0] < n_full) & (state[1] > 0), body,
                   (jnp.int32(0), jnp.int32(1)))

    for g in range(SB_LANE_GROUPS):
        o_ref[:, g * LANES:(g + 1) * LANES] = jnp.where(
            low_half, acc_ref[2 * g], acc_ref[2 * g + 1]).astype(BF16)


def _attn_kernel(sink_ref, qa_ref, kap_ref, kac_ref, vap_ref, vac_ref, qs_ref, ks_ref, vs_ref,
                 ntri_ref, ya_ref, yb_ref, qm_ref, acc_ref, carry_ref, z_ref):
    i = pl.program_id(1)
    lane = lax.broadcasted_iota(jnp.int32, (BLOCK, LANES), 1)
    low_half = lane < HEAD_DIM
    _swa_block(i, low_half, sink_ref, qa_ref, kap_ref, kac_ref, vap_ref, vac_ref, ya_ref)
    _sb_block(i, low_half, qs_ref, ks_ref, vs_ref, ntri_ref, yb_ref, qm_ref, acc_ref, carry_ref,
              z_ref)


def _attention(qa, ka, va, sinks, qs, ks, vs, ntri):
    cur = lambda b, n: (b * N_QBLOCKS + n, 0)
    prev = lambda b, n: (b * N_QBLOCKS + jnp.maximum(n - 1, 0), 0)
    seq = lambda b, n: (b, 0)
    return pl.pallas_call(
        _attn_kernel,
        grid=(BATCH, N_QBLOCKS),
        in_specs=[
            pl.BlockSpec(memory_space=pltpu.SMEM),
            pl.BlockSpec((BLOCK, D_SWA_Q), cur),
            pl.BlockSpec((BLOCK, D_SWA_KV_DUP), prev),
            pl.BlockSpec((BLOCK, D_SWA_KV_DUP), cur),
            pl.BlockSpec((BLOCK, D_SWA_KV_DUP), prev),
            pl.BlockSpec((BLOCK, D_SWA_KV_DUP), cur),
            pl.BlockSpec((BLOCK, D_SB), cur),
            pl.BlockSpec((SEQ, D_SB), seq),
            pl.BlockSpec((SEQ, D_SB), seq),
            pl.BlockSpec((SB_KEY_TILE, SB_KEY_TILE), lambda b, n: (0, 0)),
        ],
        out_specs=[pl.BlockSpec((BLOCK, D_SWA_Q), cur), pl.BlockSpec((BLOCK, D_SB), cur)],
        out_shape=[jax.ShapeDtypeStruct((TOKENS, D_SWA_Q), BF16),
                   jax.ShapeDtypeStruct((TOKENS, D_SB), BF16)],
        scratch_shapes=[pltpu.VMEM((SB_HEADS, BLOCK, LANES), BF16),
                        pltpu.VMEM((SB_HEADS, BLOCK, LANES), F32),
                        pltpu.VMEM((SB_HEADS, BLOCK, LANES), F32),
                        pltpu.VMEM((SB_HEADS, BLOCK, SB_KEY_TILE), F32)],
        compiler_params=_params(2),
        name="attention",
    )(sinks, qa, ka, ka, va, va, qs, ks, vs, ntri)


def _max4(a, b, c, d):
    return jnp.maximum(jnp.maximum(a, b), jnp.maximum(c, d))


def _first_index_of(vals, target):
    idx = jnp.full(target.shape, len(vals) - 1, jnp.int32)
    for p in range(len(vals) - 2, -1, -1):
        idx = jnp.where(vals[p] == target, p, idx)
    return idx


def _select_by_index(vals, idx):
    out = vals[-1]
    for p in range(len(vals) - 2, -1, -1):
        out = jnp.where(idx == p, vals[p], out)
    return out


def _mix_kernel(x_ref, ya_ref, yb_ref, wg_ref, bg_ref, wa_ref, wb_ref, wo_ref, lng_ref, lnb_ref,
                wrh_ref, wrl_ref, rb_ref, upper_ref,
                x1_ref, eid_ref, gate_ref, rank_ref, cnt_ref, base_ref):
    step = pl.program_id(0)
    x = x_ref[...]
    gates = jnp.dot(x.astype(BF16), wg_ref[...], preferred_element_type=F32) + bg_ref[...]
    pa = jnp.dot(ya_ref[...], wa_ref[...], preferred_element_type=F32)
    pb = jnp.dot(yb_ref[...], wb_ref[...], preferred_element_type=F32)
    merged = (jax.nn.sigmoid(gates[:, :D_MODEL]) * pa + jax.nn.sigmoid(gates[:, D_MODEL:]) * pb)
    mix = jnp.dot(merged.astype(BF16), wo_ref[...], preferred_element_type=F32)
    x1 = _layer_norm(DEEPNORM_ALPHA * x + mix, lng_ref[...], lnb_ref[...])
    _store_token_tiles(x1_ref, x1, ROW_BLOCK)

    x1h = x1.astype(BF16)
    x1l = (x1 - x1h.astype(F32)).astype(BF16)
    wrh = wrh_ref[...]
    logits = (lax.dot_general(wrh, x1h, NT_DIMS, preferred_element_type=F32)
              + lax.dot_general(wrh, x1l, NT_DIMS, preferred_element_type=F32)
              + lax.dot_general(wrl_ref[...], x1h, NT_DIMS, preferred_element_type=F32))
    aff = jax.nn.sigmoid(logits)
    reps = ROW_BLOCK // LANES
    biased = aff + jnp.concatenate([rb_ref[...]] * reps, axis=1)
    bm = [biased[p * N_GROUPS:(p + 1) * N_GROUPS, :] for p in range(EXPERTS_PER_GROUP)]
    am = [aff[p * N_GROUPS:(p + 1) * N_GROUPS, :] for p in range(EXPERTS_PER_GROUP)]
    hi1, lo1 = jnp.maximum(bm[0], bm[1]), jnp.minimum(bm[0], bm[1])
    hi2, lo2 = jnp.maximum(bm[2], bm[3]), jnp.minimum(bm[2], bm[3])
    group_score = (jnp.maximum(hi1, hi2)
                   + jnp.maximum(jnp.minimum(hi1, hi2), jnp.maximum(lo1, lo2)))
    group_iota = lax.broadcasted_iota(jnp.int32, (N_GROUPS, ROW_BLOCK), 0)
    best = jnp.max(group_score, axis=0, keepdims=True)
    g_sel = jnp.min(jnp.where(group_score == best, group_iota, N_GROUPS), axis=0, keepdims=True)
    in_sel = group_iota == g_sel
    bsel = [jnp.sum(jnp.where(in_sel, b, 0.0), axis=0, keepdims=True) for b in bm]
    asel = [jnp.sum(jnp.where(in_sel, a, 0.0), axis=0, keepdims=True) for a in am]
    i1 = _first_index_of(bsel, _max4(*bsel))
    rest = [jnp.where(i1 == p, -jnp.inf, bsel[p]) for p in range(EXPERTS_PER_GROUP)]
    i2 = _first_index_of(rest, _max4(*rest))
    a1 = _select_by_index(asel, i1)
    a2 = _select_by_index(asel, i2)
    e1 = g_sel * EXPERTS_PER_GROUP + i1
    e2 = g_sel * EXPERTS_PER_GROUP + i2
    eid_ref[...] = jnp.concatenate([e1, e2], axis=0)
    gate_ref[...] = jnp.concatenate([a1 / (a1 + a2), a2 / (a1 + a2)], axis=0)

    @pl.when(step == 0)
    def _():
        base_ref[...] = jnp.zeros_like(base_ref)

    expert_iota = lax.broadcasted_iota(jnp.int32, (N_EXPERTS, ROW_BLOCK), 0)
    oh1 = expert_iota == e1
    oh2 = expert_iota == e2
    upper = upper_ref[...]
    c1 = jnp.dot(jnp.where(oh1, 1.0, 0.0).astype(BF16), upper, preferred_element_type=F32)
    c2 = jnp.dot(jnp.where(oh2, 1.0, 0.0).astype(BF16), upper, preferred_element_type=F32)
    base = base_ref[...]
    tot1 = c1[:, ROW_BLOCK:]
    tot2 = c2[:, ROW_BLOCK:]
    before1 = jnp.concatenate([base] * reps, axis=1) + c1[:, :ROW_BLOCK]
    before2 = jnp.concatenate([base + tot1] * reps, axis=1) + c2[:, :ROW_BLOCK]
    r1 = jnp.sum(jnp.where(oh1, before1, 0.0), axis=0, keepdims=True)
    r2 = jnp.sum(jnp.where(oh2, before2, 0.0), axis=0, keepdims=True)
    rank_ref[...] = jnp.concatenate([r1, r2], axis=0).astype(jnp.int32)
    new_base = base + tot1 + tot2
    base_ref[...] = new_base
    cnt_ref[...] = new_base


def _mix(x2, ya, yb, lw, shared):
    row = lambda i: (i, 0)
    fixed = lambda i: (0, 0)
    col = lambda i: (0, i)
    full = lambda a: pl.BlockSpec(a.shape, fixed)
    weights = [lw["w_g"], lw["b_g"], lw["w_a"], lw["w_b"], lw["w_o"], lw["ln1_g"], lw["ln1_b"],
               shared["wr_hi"], shared["wr_lo"], shared["rbias"], shared["upper"]]
    return pl.pallas_call(
        _mix_kernel,
        grid=(TOKENS // ROW_BLOCK,),
        in_specs=[pl.BlockSpec((ROW_BLOCK, D_MODEL), row),
                  pl.BlockSpec((ROW_BLOCK, D_SWA_Q), row),
                  pl.BlockSpec((ROW_BLOCK, D_SB), row)] + [full(w) for w in weights],
        out_specs=[pl.BlockSpec((ROW_BLOCK * TILE_ROWS, LANES), row),
                   pl.BlockSpec((TOP_K, ROW_BLOCK), col),
                   pl.BlockSpec((TOP_K, ROW_BLOCK), col),
                   pl.BlockSpec((TOP_K, ROW_BLOCK), col),
                   pl.BlockSpec((N_EXPERTS, LANES), fixed)],
        out_shape=[jax.ShapeDtypeStruct((TOKENS * TILE_ROWS, LANES), F32),
                   jax.ShapeDtypeStruct((TOP_K, TOKENS), jnp.int32),
                   jax.ShapeDtypeStruct((TOP_K, TOKENS), F32),
                   jax.ShapeDtypeStruct((TOP_K, TOKENS), jnp.int32),
                   jax.ShapeDtypeStruct((N_EXPERTS, LANES), F32)],
        scratch_shapes=[pltpu.VMEM((N_EXPERTS, LANES), F32)],
        compiler_params=_params(1),
        name="mix_ln_route",
    )(x2, ya, yb, *weights)


def _token_tile(ref, row):
    return ref.at[pl.ds(pl.multiple_of(row * TILE_ROWS, TILE_ROWS), TILE_ROWS)]


def _dispatch_kernel(dest_ref, x1_ref, xs_in_hbm, xs_hbm, stage_ref, sems):
    del xs_in_hbm
    step = pl.program_id(0)
    n_steps = pl.num_programs(0)
    slot = step % 2
    stage_ref[slot] = x1_ref[...]

    def issue(t, c):
        for k in range(TOP_K):
            pltpu.make_async_copy(_token_tile(stage_ref.at[slot], t),
                                  _token_tile(xs_hbm, dest_ref[TOP_K * t + k]),
                                  sems.at[slot]).start()
        return c

    lax.fori_loop(0, DISPATCH_ROWS, issue, 0, unroll=8)

    def drain(slot_):
        def body(t, c):
            for k in range(TOP_K):
                pltpu.make_async_copy(_token_tile(stage_ref.at[slot_], 0), _token_tile(xs_hbm, 0),
                                      sems.at[slot_]).wait()
            return c
        lax.fori_loop(0, DISPATCH_ROWS, body, 0, unroll=8)

    @pl.when(step > 0)
    def _():
        drain(1 - slot)

    @pl.when(step == n_steps - 1)
    def _():
        drain(slot)


def _dispatch(dest, x1, xs):
    return pl.pallas_call(
        _dispatch_kernel,
        grid=(TOKENS // DISPATCH_ROWS,),
        in_specs=[pl.BlockSpec((TOP_K * DISPATCH_ROWS,), lambda i: (i,), memory_space=pltpu.SMEM),
                  pl.BlockSpec((DISPATCH_ROWS * TILE_ROWS, LANES), lambda i: (i, 0)),
                  pl.BlockSpec(memory_space=pl.ANY)],
        out_specs=pl.BlockSpec(memory_space=pl.ANY),
        out_shape=jax.ShapeDtypeStruct(xs.shape, xs.dtype),
        scratch_shapes=[pltpu.VMEM((2, DISPATCH_ROWS * TILE_ROWS, LANES), F32),
                        pltpu.SemaphoreType.DMA((2,))],
        input_output_aliases={2: 0},
        compiler_params=_params(1),
        name="dispatch",
    )(dest, x1, xs)


def _expert_kernel(be_ref, nu_ref, x_ref, wg_ref, wu_ref, wd_ref, y_ref, wg_bf, wu_bf, wd_bf):
    i = pl.program_id(0)

    @pl.when(i < nu_ref[0])
    def _():
        e = be_ref[i]
        e_prev = be_ref[jnp.maximum(i - 1, 0)]

        @pl.when((i == 0) | (e != e_prev))
        def _():
            wg_bf[...] = wg_ref[...].astype(BF16)
            wu_bf[...] = wu_ref[...].astype(BF16)
            wd_bf[...] = wd_ref[...].astype(BF16)

        xb = _load_token_tiles(x_ref, EXPERT_ROWS).astype(BF16)
        hg = jnp.dot(xb, wg_bf[...], preferred_element_type=F32)
        hu = jnp.dot(xb, wu_bf[...], preferred_element_type=F32)
        h = hg * jax.nn.sigmoid(hg) * hu
        y = jnp.dot(h.astype(BF16), wd_bf[...], preferred_element_type=F32)
        _store_token_tiles(y_ref, y, EXPERT_ROWS)

    @pl.when(i >= nu_ref[0])
    def _():
        y_ref[...] = jnp.zeros_like(y_ref)


def _experts(layer, block_expert, n_used, xs, w_gate, w_up, w_down):
    blk = lambda i, be, nu: (jnp.minimum(i, nu[0] - 1), 0)
    out_blk = lambda i, be, nu: (i, 0)
    wsel = lambda i, be, nu: (layer, be[jnp.minimum(i, nu[0] - 1)], 0, 0)
    grid_spec = pltpu.PrefetchScalarGridSpec(
        num_scalar_prefetch=2,
        grid=(N_EXPERT_BLOCKS,),
        in_specs=[pl.BlockSpec((EXPERT_ROWS * TILE_ROWS, LANES), blk),
                  pl.BlockSpec((None, None, D_MODEL, D_EXPERT), wsel),
                  pl.BlockSpec((None, None, D_MODEL, D_EXPERT), wsel),
                  pl.BlockSpec((None, None, D_EXPERT, D_MODEL), wsel)],
        out_specs=pl.BlockSpec((EXPERT_ROWS * TILE_ROWS, LANES), out_blk),
        scratch_shapes=[pltpu.VMEM((D_MODEL, D_EXPERT), BF16),
                        pltpu.VMEM((D_MODEL, D_EXPERT), BF16),
                        pltpu.VMEM((D_EXPERT, D_MODEL), BF16)],
    )
    return pl.pallas_call(
        _expert_kernel,
        grid_spec=grid_spec,
        out_shape=jax.ShapeDtypeStruct((N_EXPERT_BLOCKS * EXPERT_ROWS * TILE_ROWS, LANES), F32),
        compiler_params=_params(1),
        name="experts",
    )(block_expert, n_used, xs, w_gate, w_up, w_down)


def _combine_kernel(dest_ref, dest_next_ref, x1_ref, gate_ref, lng_ref, lnb_ref, ys_hbm,
                    x2_ref, buf_ref, sems):
    step = pl.program_id(0)
    n_steps = pl.num_programs(0)
    slot = step % 2

    def issue(d_ref, slot_):
        def body(t, c):
            for k in range(TOP_K):
                pltpu.make_async_copy(_token_tile(ys_hbm, d_ref[TOP_K * t + k]),
                                      _token_tile(buf_ref.at[slot_, k], t), sems.at[slot_]).start()
            return c
        lax.fori_loop(0, GATHER_ROWS, body, 0, unroll=8)

    @pl.when(step == 0)
    def _():
        issue(dest_ref, 0)

    @pl.when(step + 1 < n_steps)
    def _():
        issue(dest_next_ref, 1 - slot)

    def drain(t, c):
        for k in range(TOP_K):
            pltpu.make_async_copy(_token_tile(ys_hbm, 0), _token_tile(buf_ref.at[slot, k], 0),
                                  sems.at[slot]).wait()
        return c

    lax.fori_loop(0, GATHER_ROWS, drain, 0, unroll=8)
    g = gate_ref[...]
    y = (_load_token_tiles(buf_ref.at[slot, 0], GATHER_ROWS) * g[:, 0:1]
         + _load_token_tiles(buf_ref.at[slot, 1], GATHER_ROWS) * g[:, 1:2])
    x1 = _load_token_tiles(x1_ref, GATHER_ROWS)
    x2_ref[...] = _layer_norm(DEEPNORM_ALPHA * x1 + y, lng_ref[...], lnb_ref[...])


def _combine(dest, x1, gate_cols, ln_g, ln_b, ys):
    n_steps = TOKENS // GATHER_ROWS
    row = lambda i: (i, 0)
    fixed = lambda i: (0, 0)
    return pl.pallas_call(
        _combine_kernel,
        grid=(n_steps,),
        in_specs=[pl.BlockSpec((TOP_K * GATHER_ROWS,), lambda i: (i,), memory_space=pltpu.SMEM),
                  pl.BlockSpec((TOP_K * GATHER_ROWS,), lambda i: (jnp.minimum(i + 1, n_steps - 1),),
                               memory_space=pltpu.SMEM),
                  pl.BlockSpec((GATHER_ROWS * TILE_ROWS, LANES), row),
                  pl.BlockSpec((GATHER_ROWS, TOP_K), row),
                  pl.BlockSpec((1, D_MODEL), fixed),
                  pl.BlockSpec((1, D_MODEL), fixed),
                  pl.BlockSpec(memory_space=pl.ANY)],
        out_specs=pl.BlockSpec((GATHER_ROWS, D_MODEL), row),
        out_shape=jax.ShapeDtypeStruct((TOKENS, D_MODEL), F32),
        scratch_shapes=[pltpu.VMEM((2, TOP_K, GATHER_ROWS * TILE_ROWS, LANES), F32),
                        pltpu.SemaphoreType.DMA((2,))],
        compiler_params=_params(1),
        name="combine_ln",
    )(dest, dest, x1, gate_cols, ln_g, ln_b, ys)


def _prepare_layer(l, w_in, b_in, w_branch_a, w_branch_b, w_out, ln1_g, ln1_b):
    def dup_heads(lo):
        h = jnp.arange(HEAD_DIM)
        return jnp.concatenate([lo + kv * HEAD_DIM + h for kv in range(SWA_KV_HEADS) for _ in range(2)])

    scale = HEAD_DIM ** -0.5
    q_a = jnp.arange(0, 512)
    k_a = dup_heads(512)
    v_a = dup_heads(640)
    q_s = jnp.arange(768, 1280)
    kv_s = jnp.arange(1280, 2304)
    cols = jnp.concatenate([q_a, k_a, v_a, q_s, kv_s])
    col_scale = jnp.concatenate([jnp.full((512,), scale), jnp.ones((512,)),
                                 jnp.full((512,), scale), jnp.ones((1024,))]).astype(F32)
    w, b = w_in[l], b_in[l]
    return {
        "w_qkv": (w[:, cols] * col_scale).astype(BF16),
        "b_qkv": (b[cols] * col_scale)[None, :],
        "w_g": w[:, 2304:].astype(BF16),
        "b_g": b[None, 2304:],
        "w_a": w_branch_a[l].astype(BF16),
        "w_b": w_branch_b[l].astype(BF16),
        "w_o": w_out[l].astype(BF16),
        "ln1_g": ln1_g[l][None, :],
        "ln1_b": ln1_b[l][None, :],
    }


def _prepare_shared(w_router, router_bias):
    r = jnp.arange(N_EXPERTS)
    perm = (r % N_GROUPS) * EXPERTS_PER_GROUP + r // N_GROUPS
    wr_t = w_router.T[perm].astype(F32)
    wr_hi = wr_t.astype(BF16)
    wr_lo = (wr_t - wr_hi.astype(F32)).astype(BF16)
    rbias = jnp.broadcast_to(router_bias.astype(F32)[perm][:, None], (N_EXPERTS, LANES))
    t = jnp.arange(ROW_BLOCK)
    strict_upper = (t[:, None] < t[None, :]).astype(BF16)
    upper = jnp.concatenate([strict_upper, jnp.ones((ROW_BLOCK, LANES), BF16)], axis=1)
    k = jnp.arange(SB_KEY_TILE)
    ntri = -(k[:, None] >= k[None, :]).astype(BF16)
    return {"wr_hi": wr_hi, "wr_lo": wr_lo, "rbias": rbias, "upper": upper, "ntri": ntri}


def _block_plan(counts, eid, rank):
    n_blocks = (counts + EXPERT_ROWS - 1) // EXPERT_ROWS
    block_end = jnp.cumsum(n_blocks)
    row_start = (block_end - n_blocks) * EXPERT_ROWS
    n_used = block_end[-1:].astype(jnp.int32)
    blocks = jnp.arange(N_EXPERT_BLOCKS)
    block_expert = jnp.minimum(jnp.sum(block_end[None, :] <= blocks[:, None], axis=1),
                               N_EXPERTS - 1).astype(jnp.int32)
    experts = jnp.arange(N_EXPERTS)
    start_of = jnp.sum(jnp.where(eid[:, :, None] == experts, row_start, 0), axis=-1)
    dest = (start_of + rank).astype(jnp.int32)
    return block_expert, n_used, dest.T.reshape(-1)


def kernel(x, w_in, b_in, attn_sinks, w_branch_a, w_branch_b, w_out, ln1_g, ln1_b, w_router,
           router_bias, w_gate, w_up, w_down, ln2_g, ln2_b):
    shared = _prepare_shared(w_router, router_bias)
    x2 = x.reshape(TOKENS, D_MODEL)
    xs = jnp.zeros((N_EXPERT_BLOCKS * EXPERT_ROWS * TILE_ROWS, LANES), F32)
    for l in range(DEPTH):
        lw = _prepare_layer(l, w_in, b_in, w_branch_a, w_branch_b, w_out, ln1_g, ln1_b)
        qa, ka, va, qs, ks, vs = _inproj(x2, lw["w_qkv"], lw["b_qkv"])
        ya, yb = _attention(qa, ka, va, attn_sinks[l], qs, ks, vs, shared["ntri"])
        x1, eid, gate, rank, cnt = _mix(x2, ya, yb, lw, shared)
        block_expert, n_used, dest = _block_plan(cnt[:, 0].astype(jnp.int32), eid, rank)
        xs = _dispatch(dest, x1, xs)
        ys = _experts(l, block_expert, n_used, xs, w_gate, w_up, w_down)
        x2 = _combine(dest, x1, gate.T, ln2_g[l][None, :], ln2_b[l][None, :], ys)
    return x2.reshape(BATCH, SEQ, D_MODEL)
```

```python
import jax
import jax.numpy as jnp
from jax import lax
from jax.experimental import pallas as pl
from jax.experimental.pallas import tpu as pltpu

D_MODEL = 1024
BATCH = 16
SEQ = 2048
DEPTH = 4
HEAD_DIM = 64
SWA_Q_HEADS = 8
SWA_KV_HEADS = 2
SWA_GROUP = SWA_Q_HEADS // SWA_KV_HEADS
SB_HEADS = 8
BLOCK = 128
N_EXPERTS = 32
N_GROUPS = 8
EXPERTS_PER_GROUP = N_EXPERTS // N_GROUPS
TOP_K = 2
D_EXPERT = 512
LN_EPS = 1e-5
DEEPNORM_ALPHA = (2 * DEPTH) ** 0.25

TOKENS = BATCH * SEQ
N_QBLOCKS = SEQ // BLOCK
LANES = 128
SUBLANES = 8
TILE_ROWS = D_MODEL // LANES
assert TILE_ROWS == SUBLANES
D_SWA_Q = SWA_Q_HEADS * HEAD_DIM
D_SWA_KV_DUP = 2 * SWA_KV_HEADS * HEAD_DIM
D_SB = SB_HEADS * HEAD_DIM
SB_LANE_GROUPS = D_SB // LANES
QKV_COLS = D_SWA_Q + 2 * D_SWA_KV_DUP + 3 * D_SB

ROW_BLOCK = 512
SB_KEY_TILE = 256
SB_PIPE_LAG = 2
SB_DEAD_LOG = -110.0
EXPERT_ROWS = 512
N_EXPERT_BLOCKS = (TOKENS * TOP_K) // EXPERT_ROWS + N_EXPERTS
DISPATCH_ROWS = 256
GATHER_ROWS = 256
VMEM_LIMIT = 56 * 1024 * 1024
LOG2E = 1.4426950408889634

BF16 = jnp.bfloat16
F32 = jnp.float32
NT_DIMS = (((1,), (1,)), ((), ()))


def _params(n_axes, vmem=VMEM_LIMIT):
    return pltpu.CompilerParams(dimension_semantics=("arbitrary",) * n_axes,
                                vmem_limit_bytes=vmem)


def _layer_norm(h, g, b):
    mu = jnp.mean(h, axis=-1, keepdims=True)
    c = h - mu
    var = jnp.mean(c * c, axis=-1, keepdims=True)
    return c * lax.rsqrt(var + LN_EPS) * g + b


def _load_token_tiles(ref, n_tokens):
    return jnp.concatenate(
        [ref[pl.ds(j, n_tokens, stride=TILE_ROWS), :] for j in range(TILE_ROWS)], axis=1)


def _store_token_tiles(ref, value, n_tokens):
    for j in range(TILE_ROWS):
        ref[pl.ds(j, n_tokens, stride=TILE_ROWS), :] = value[:, j * LANES:(j + 1) * LANES]


_QKV_SLICES = ((0, 512), (512, 768), (768, 1024), (1024, 1536), (1536, 2048), (2048, 2560))


def _inproj_kernel(x_ref, w_ref, b_ref, *out_refs):
    xb = x_ref[...].astype(BF16)
    for (lo, hi), o_ref in zip(_QKV_SLICES, out_refs):
        p = jnp.dot(xb, w_ref[:, lo:hi], preferred_element_type=F32) + b_ref[:, lo:hi]
        o_ref[...] = p.astype(BF16)


def _inproj(x2, w_qkv, b_qkv):
    widths = [hi - lo for lo, hi in _QKV_SLICES]
    return pl.pallas_call(
        _inproj_kernel,
        grid=(TOKENS // ROW_BLOCK,),
        in_specs=[
            pl.BlockSpec((ROW_BLOCK, D_MODEL), lambda i: (i, 0)),
            pl.BlockSpec((D_MODEL, QKV_COLS), lambda i: (0, 0)),
            pl.BlockSpec((1, QKV_COLS), lambda i: (0, 0)),
        ],
        out_specs=[pl.BlockSpec((ROW_BLOCK, w), lambda i: (i, 0)) for w in widths],
        out_shape=[jax.ShapeDtypeStruct((TOKENS, w), BF16) for w in widths],
        compiler_params=_params(1),
        name="inproj",
    )(x2, w_qkv, b_qkv)


def _swa_block(n, low_half, sink_ref, q_ref, kp_ref, kc_ref, vp_ref, vc_ref, o_ref):
    qi = lax.broadcasted_iota(jnp.int32, (BLOCK, 2 * BLOCK), 0)
    kj = lax.broadcasted_iota(jnp.int32, (BLOCK, 2 * BLOCK), 1)
    dist = qi + BLOCK - kj
    valid = (dist >= 0) & (dist < BLOCK) & ((kj >= BLOCK) | (n > 0))
    dist_f = dist.astype(F32)
    zero = jnp.zeros((), BF16)

    def window(ref_prev, ref_cur, kvh):
        return jnp.concatenate([ref_prev[:, kvh * LANES:(kvh + 1) * LANES],
                                ref_cur[:, kvh * LANES:(kvh + 1) * LANES]], axis=0)

    scores = []
    for head in range(SWA_Q_HEADS):
        grp = head // 2
        q = q_ref[:, grp * LANES:(grp + 1) * LANES]
        qm = jnp.where(low_half, q, zero) if head % 2 == 0 else jnp.where(low_half, zero, q)
        scores.append(lax.dot_general(qm, window(kp_ref, kc_ref, head // SWA_GROUP), NT_DIMS,
                                      preferred_element_type=F32))
    outs = []
    for head in range(SWA_Q_HEADS):
        slope = 2.0 ** (-8.0 * (head + 1) / SWA_Q_HEADS)
        sink = sink_ref[head]
        s = jnp.where(valid, scores[head] - slope * dist_f, -jnp.inf)
        m = jnp.maximum(jnp.max(s, axis=-1, keepdims=True), sink)
        p = jnp.exp(s - m)
        denom = jnp.sum(p, axis=-1, keepdims=True) + jnp.exp(sink - m)
        pv = jnp.dot(p.astype(BF16), window(vp_ref, vc_ref, head // SWA_GROUP),
                     preferred_element_type=F32)
        outs.append(pv / denom)
    for grp in range(D_SWA_Q // LANES):
        o_ref[:, grp * LANES:(grp + 1) * LANES] = jnp.where(
            low_half, outs[2 * grp], outs[2 * grp + 1]).astype(BF16)


def _sb_block(i, low_half, q_ref, k_ref, v_ref, ntri_ref, o_ref, qm_ref, acc_ref, carry_ref, z_ref):
    zero = jnp.zeros((), BF16)
    for h in range(SB_HEADS):
        q = q_ref[:, (h // 2) * LANES:(h // 2 + 1) * LANES]
        qm_ref[h] = jnp.where(low_half, q, zero) if h % 2 == 0 else jnp.where(low_half, zero, q)
    acc_ref[...] = jnp.zeros_like(acc_ref)
    carry_ref[...] = jnp.zeros_like(carry_ref)
    sign_bit = jnp.uint32(0x80000000)
    odd = i % 2
    full_end = (i - odd) * BLOCK
    n_full = i // 2

    def scores(h, start, width):
        g = h // 2
        k = k_ref[pl.ds(start, width), g * LANES:(g + 1) * LANES]
        return lax.dot_general(qm_ref[h], k, NT_DIMS, preferred_element_type=F32)

    def sweep(load_z, start, width, tiles_done, strict=None):
        next_start = pl.multiple_of(
            jnp.maximum(full_end - (tiles_done + 1) * SB_KEY_TILE, 0), BLOCK)
        ntri = ntri_ref[:width, :width]
        suffixes = {}
        for step in range(SB_HEADS + SB_PIPE_LAG):
            if step < SB_HEADS:
                z = load_z(step)
                neg_abs = lax.bitcast_convert_type(
                    lax.bitcast_convert_type(z, jnp.uint32) | sign_bit, F32)
                sp = jnp.maximum(z, 0.0) + jnp.log(1.0 + jnp.exp2(neg_abs * LOG2E))
                if strict is not None:
                    sp = jnp.where(strict, sp, 0.0)
                suffixes[step] = jnp.dot(sp.astype(BF16), ntri, preferred_element_type=F32)
            h = step - SB_PIPE_LAG
            if h >= 0:
                suffix = suffixes.pop(h)
                carry = carry_ref[h]
                carry_w = jnp.concatenate([carry] * (width // LANES), axis=1)
                a = jnp.exp2((load_z(h) + suffix + carry_w) * LOG2E)
                if strict is not None:
                    a = jnp.where(strict, a, 0.0)
                g = h // 2
                v = v_ref[pl.ds(start, width), g * LANES:(g + 1) * LANES]
                acc_ref[h] += jnp.dot(a.astype(BF16), v, preferred_element_type=F32)
                carry_ref[h] = carry + jnp.broadcast_to(suffix[:, 0:1], (BLOCK, LANES))
                z_ref[h] = scores(h, next_start, SB_KEY_TILE)

    @pl.when(odd == 0)
    def _():
        start = pl.multiple_of(i * BLOCK, BLOCK)
        qpos = lax.broadcasted_iota(jnp.int32, (BLOCK, BLOCK), 0)
        kpos = lax.broadcasted_iota(jnp.int32, (BLOCK, BLOCK), 1)
        diag = [scores(h, start, BLOCK) for h in range(SB_HEADS)]
        sweep(lambda h: diag[h], start, BLOCK, 0, kpos < qpos)

    @pl.when(odd == 1)
    def _():
        start = pl.multiple_of((i - 1) * BLOCK, BLOCK)
        qpos = lax.broadcasted_iota(jnp.int32, (BLOCK, SB_KEY_TILE), 0) + BLOCK
        kpos = lax.broadcasted_iota(jnp.int32, (BLOCK, SB_KEY_TILE), 1)
        for h in range(SB_HEADS):
            z_ref[h] = scores(h, start, SB_KEY_TILE)
        sweep(lambda h: z_ref[h], start, SB_KEY_TILE, 0, kpos < qpos)

    def body(state):
        j, _ = state
        start = pl.multiple_of(full_end - (j + 1) * SB_KEY_TILE, BLOCK)
        sweep(lambda h: z_ref[h], start, SB_KEY_TILE, j + 1)
        return j + 1, (jnp.max(carry_ref[...]) > SB_DEAD_LOG).astype(jnp.int32)

    lax.while_loop(lambda state: (state[0] < n_full) & (state[1] > 0), body,
                   (jnp.int32(0), jnp.int32(1)))

    for g in range(SB_LANE_GROUPS):
        o_ref[:, g * LANES:(g + 1) * LANES] = jnp.where(
            low_half, acc_ref[2 * g], acc_ref[2 * g + 1]).astype(BF16)


def _attn_kernel(sink_ref, qa_ref, ka_ref, va_ref, qs_ref, ks_ref, vs_ref, ntri_ref,
                 ya_ref, yb_ref, qm_ref, acc_ref, carry_ref, z_ref):
    lane = lax.broadcasted_iota(jnp.int32, (BLOCK, LANES), 1)
    low_half = lane < HEAD_DIM

    def query_block(i, c):
        rows = pl.ds(pl.multiple_of(i * BLOCK, BLOCK), BLOCK)
        prev = pl.ds(pl.multiple_of(jnp.maximum(i - 1, 0) * BLOCK, BLOCK), BLOCK)
        _swa_block(i, low_half, sink_ref, qa_ref.at[rows], ka_ref.at[prev], ka_ref.at[rows],
                   va_ref.at[prev], va_ref.at[rows], ya_ref.at[rows])
        _sb_block(i, low_half, qs_ref.at[rows], ks_ref, vs_ref, ntri_ref, yb_ref.at[rows], qm_ref,
                  acc_ref, carry_ref, z_ref)
        return c

    lax.fori_loop(0, N_QBLOCKS, query_block, 0)


def _attention(qa, ka, va, sinks, qs, ks, vs, ntri):
    seq = lambda b: (b, 0)
    return pl.pallas_call(
        _attn_kernel,
        grid=(BATCH,),
        in_specs=[
            pl.BlockSpec(memory_space=pltpu.SMEM),
            pl.BlockSpec((SEQ, D_SWA_Q), seq),
            pl.BlockSpec((SEQ, D_SWA_KV_DUP), seq),
            pl.BlockSpec((SEQ, D_SWA_KV_DUP), seq),
            pl.BlockSpec((SEQ, D_SB), seq),
            pl.BlockSpec((SEQ, D_SB), seq),
            pl.BlockSpec((SEQ, D_SB), seq),
            pl.BlockSpec((SB_KEY_TILE, SB_KEY_TILE), lambda b: (0, 0)),
        ],
        out_specs=[pl.BlockSpec((SEQ, D_SWA_Q), seq), pl.BlockSpec((SEQ, D_SB), seq)],
        out_shape=[jax.ShapeDtypeStruct((TOKENS, D_SWA_Q), BF16),
                   jax.ShapeDtypeStruct((TOKENS, D_SB), BF16)],
        scratch_shapes=[pltpu.VMEM((SB_HEADS, BLOCK, LANES), BF16),
                        pltpu.VMEM((SB_HEADS, BLOCK, LANES), F32),
                        pltpu.VMEM((SB_HEADS, BLOCK, LANES), F32),
                        pltpu.VMEM((SB_HEADS, BLOCK, SB_KEY_TILE), F32)],
        compiler_params=_params(1),
        name="attention",
    )(sinks, qa, ka, va, qs, ks, vs, ntri)


def _max4(a, b, c, d):
    return jnp.maximum(jnp.maximum(a, b), jnp.maximum(c, d))


def _first_index_of(vals, target):
    idx = jnp.full(target.shape, len(vals) - 1, jnp.int32)
    for p in range(len(vals) - 2, -1, -1):
        idx = jnp.where(vals[p] == target, p, idx)
    return idx


def _select_by_index(vals, idx):
    out = vals[-1]
    for p in range(len(vals) - 2, -1, -1):
        out = jnp.where(idx == p, vals[p], out)
    return out


def _mix_kernel(x_ref, ya_ref, yb_ref, wg_ref, bg_ref, wa_ref, wb_ref, wo_ref, lng_ref, lnb_ref,
                wrh_ref, wrl_ref, rb_ref, upper_ref,
                x1_ref, eid_ref, gate_ref, rank_ref, cnt_ref, base_ref):
    step = pl.program_id(0)
    x = x_ref[...]
    gates = jnp.dot(x.astype(BF16), wg_ref[...], preferred_element_type=F32) + bg_ref[...]
    pa = jnp.dot(ya_ref[...], wa_ref[...], preferred_element_type=F32)
    pb = jnp.dot(yb_ref[...], wb_ref[...], preferred_element_type=F32)
    merged = (jax.nn.sigmoid(gates[:, :D_MODEL]) * pa + jax.nn.sigmoid(gates[:, D_MODEL:]) * pb)
    mix = jnp.dot(merged.astype(BF16), wo_ref[...], preferred_element_type=F32)
    x1 = _layer_norm(DEEPNORM_ALPHA * x + mix, lng_ref[...], lnb_ref[...])
    _store_token_tiles(x1_ref, x1, ROW_BLOCK)

    x1h = x1.astype(BF16)
    x1l = (x1 - x1h.astype(F32)).astype(BF16)
    wrh = wrh_ref[...]
    logits = (lax.dot_general(wrh, x1h, NT_DIMS, preferred_element_type=F32)
              + lax.dot_general(wrh, x1l, NT_DIMS, preferred_element_type=F32)
              + lax.dot_general(wrl_ref[...], x1h, NT_DIMS, preferred_element_type=F32))
    aff = jax.nn.sigmoid(logits)
    reps = ROW_BLOCK // LANES
    biased = aff + jnp.concatenate([rb_ref[...]] * reps, axis=1)
    bm = [biased[p * N_GROUPS:(p + 1) * N_GROUPS, :] for p in range(EXPERTS_PER_GROUP)]
    am = [aff[p * N_GROUPS:(p + 1) * N_GROUPS, :] for p in range(EXPERTS_PER_GROUP)]
    hi1, lo1 = jnp.maximum(bm[0], bm[1]), jnp.minimum(bm[0], bm[1])
    hi2, lo2 = jnp.maximum(bm[2], bm[3]), jnp.minimum(bm[2], bm[3])
    group_score = (jnp.maximum(hi1, hi2)
                   + jnp.maximum(jnp.minimum(hi1, hi2), jnp.maximum(lo1, lo2)))
    group_iota = lax.broadcasted_iota(jnp.int32, (N_GROUPS, ROW_BLOCK), 0)
    best = jnp.max(group_score, axis=0, keepdims=True)
    g_sel = jnp.min(jnp.where(group_score == best, group_iota, N_GROUPS), axis=0, keepdims=True)
    in_sel = group_iota == g_sel
    bsel = [jnp.sum(jnp.where(in_sel, b, 0.0), axis=0, keepdims=True) for b in bm]
    asel = [jnp.sum(jnp.where(in_sel, a, 0.0), axis=0, keepdims=True) for a in am]
    i1 = _first_index_of(bsel, _max4(*bsel))
    rest = [jnp.where(i1 == p, -jnp.inf, bsel[p]) for p in range(EXPERTS_PER_GROUP)]
    i2 = _first_index_of(rest, _max4(*rest))
    a1 = _select_by_index(asel, i1)
    a2 = _select_by_index(asel, i2)
    e1 = g_sel * EXPERTS_PER_GROUP + i1
    e2 = g_sel * EXPERTS_PER_GROUP + i2
    eid_ref[...] = jnp.concatenate([e1, e2], axis=0)
    gate_ref[...] = jnp.concatenate([a1 / (a1 + a2), a2 / (a1 + a2)], axis=0)

    @pl.when(step == 0)
    def _():
        base_ref[...] = jnp.zeros_like(base_ref)

    expert_iota = lax.broadcasted_iota(jnp.int32, (N_EXPERTS, ROW_BLOCK), 0)
    oh1 = expert_iota == e1
    oh2 = expert_iota == e2
    upper = upper_ref[...]
    c1 = jnp.dot(jnp.where(oh1, 1.0, 0.0).astype(BF16), upper, preferred_element_type=F32)
    c2 = jnp.dot(jnp.where(oh2, 1.0, 0.0).astype(BF16), upper, preferred_element_type=F32)
    base = base_ref[...]
    tot1 = c1[:, ROW_BLOCK:]
    tot2 = c2[:, ROW_BLOCK:]
    before1 = jnp.concatenate([base] * reps, axis=1) + c1[:, :ROW_BLOCK]
    before2 = jnp.concatenate([base + tot1] * reps, axis=1) + c2[:, :ROW_BLOCK]
    r1 = jnp.sum(jnp.where(oh1, before1, 0.0), axis=0, keepdims=True)
    r2 = jnp.sum(jnp.where(oh2, before2, 0.0), axis=0, keepdims=True)
    rank_ref[...] = jnp.concatenate([r1, r2], axis=0).astype(jnp.int32)
    new_base = base + tot1 + tot2
    base_ref[...] = new_base
    cnt_ref[...] = new_base


def _mix(x2, ya, yb, lw, shared):
    row = lambda i: (i, 0)
    fixed = lambda i: (0, 0)
    col = lambda i: (0, i)
    full = lambda a: pl.BlockSpec(a.shape, fixed)
    weights = [lw["w_g"], lw["b_g"], lw["w_a"], lw["w_b"], lw["w_o"], lw["ln1_g"], lw["ln1_b"],
               shared["wr_hi"], shared["wr_lo"], shared["rbias"], shared["upper"]]
    return pl.pallas_call(
        _mix_kernel,
        grid=(TOKENS // ROW_BLOCK,),
        in_specs=[pl.BlockSpec((ROW_BLOCK, D_MODEL), row),
                  pl.BlockSpec((ROW_BLOCK, D_SWA_Q), row),
                  pl.BlockSpec((ROW_BLOCK, D_SB), row)] + [full(w) for w in weights],
        out_specs=[pl.BlockSpec((ROW_BLOCK * TILE_ROWS, LANES), row),
                   pl.BlockSpec((TOP_K, ROW_BLOCK), col),
                   pl.BlockSpec((TOP_K, ROW_BLOCK), col),
                   pl.BlockSpec((TOP_K, ROW_BLOCK), col),
                   pl.BlockSpec((N_EXPERTS, LANES), fixed)],
        out_shape=[jax.ShapeDtypeStruct((TOKENS * TILE_ROWS, LANES), F32),
                   jax.ShapeDtypeStruct((TOP_K, TOKENS), jnp.int32),
                   jax.ShapeDtypeStruct((TOP_K, TOKENS), F32),
                   jax.ShapeDtypeStruct((TOP_K, TOKENS), jnp.int32),
                   jax.ShapeDtypeStruct((N_EXPERTS, LANES), F32)],
        scratch_shapes=[pltpu.VMEM((N_EXPERTS, LANES), F32)],
        compiler_params=_params(1),
        name="mix_ln_route",
    )(x2, ya, yb, *weights)


def _token_tile(ref, row):
    return ref.at[pl.ds(pl.multiple_of(row * TILE_ROWS, TILE_ROWS), TILE_ROWS)]


def _dispatch_kernel(dest_ref, x1_ref, xs_in_hbm, xs_hbm, stage_ref, sems):
    del xs_in_hbm
    step = pl.program_id(0)
    n_steps = pl.num_programs(0)
    slot = step % 2
    stage_ref[slot] = x1_ref[...]

    def issue(t, c):
        for k in range(TOP_K):
            pltpu.make_async_copy(_token_tile(stage_ref.at[slot], t),
                                  _token_tile(xs_hbm, dest_ref[TOP_K * t + k]),
                                  sems.at[slot]).start(priority=k % 2)
        return c

    lax.fori_loop(0, DISPATCH_ROWS, issue, 0, unroll=8)

    def drain(slot_):
        def body(t, c):
            for k in range(TOP_K):
                pltpu.make_async_copy(_token_tile(stage_ref.at[slot_], 0), _token_tile(xs_hbm, 0),
                                      sems.at[slot_]).wait()
            return c
        lax.fori_loop(0, DISPATCH_ROWS, body, 0, unroll=8)

    @pl.when(step > 0)
    def _():
        drain(1 - slot)

    @pl.when(step == n_steps - 1)
    def _():
        drain(slot)


def _dispatch(dest, x1, xs):
    return pl.pallas_call(
        _dispatch_kernel,
        grid=(TOKENS // DISPATCH_ROWS,),
        in_specs=[pl.BlockSpec((TOP_K * DISPATCH_ROWS,), lambda i: (i,), memory_space=pltpu.SMEM),
                  pl.BlockSpec((DISPATCH_ROWS * TILE_ROWS, LANES), lambda i: (i, 0)),
                  pl.BlockSpec(memory_space=pl.ANY)],
        out_specs=pl.BlockSpec(memory_space=pl.ANY),
        out_shape=jax.ShapeDtypeStruct(xs.shape, xs.dtype),
        scratch_shapes=[pltpu.VMEM((2, DISPATCH_ROWS * TILE_ROWS, LANES), F32),
                        pltpu.SemaphoreType.DMA((2,))],
        input_output_aliases={2: 0},
        compiler_params=_params(1),
        name="dispatch",
    )(dest, x1, xs)


def _expert_kernel(be_ref, nu_ref, x_ref, wg_ref, wu_ref, wd_ref, y_ref, wg_bf, wu_bf, wd_bf):
    i = pl.program_id(0)

    @pl.when(i < nu_ref[0])
    def _():
        e = be_ref[i]
        e_prev = be_ref[jnp.maximum(i - 1, 0)]

        @pl.when((i == 0) | (e != e_prev))
        def _():
            wg_bf[...] = wg_ref[...].astype(BF16)
            wu_bf[...] = wu_ref[...].astype(BF16)
            wd_bf[...] = wd_ref[...].astype(BF16)

        xb = _load_token_tiles(x_ref, EXPERT_ROWS).astype(BF16)
        hg = jnp.dot(xb, wg_bf[...], preferred_element_type=F32)
        hu = jnp.dot(xb, wu_bf[...], preferred_element_type=F32)
        h = hg * jax.nn.sigmoid(hg) * hu
        y = jnp.dot(h.astype(BF16), wd_bf[...], preferred_element_type=F32)
        _store_token_tiles(y_ref, y, EXPERT_ROWS)

    @pl.when(i >= nu_ref[0])
    def _():
        y_ref[...] = jnp.zeros_like(y_ref)


def _experts(layer, block_expert, n_used, xs, w_gate, w_up, w_down):
    blk = lambda i, be, nu: (jnp.minimum(i, nu[0] - 1), 0)
    out_blk = lambda i, be, nu: (i, 0)
    wsel = lambda i, be, nu: (layer, be[jnp.minimum(i, nu[0] - 1)], 0, 0)
    grid_spec = pltpu.PrefetchScalarGridSpec(
        num_scalar_prefetch=2,
        grid=(N_EXPERT_BLOCKS,),
        in_specs=[pl.BlockSpec((EXPERT_ROWS * TILE_ROWS, LANES), blk),
                  pl.BlockSpec((None, None, D_MODEL, D_EXPERT), wsel),
                  pl.BlockSpec((None, None, D_MODEL, D_EXPERT), wsel),
                  pl.BlockSpec((None, None, D_EXPERT, D_MODEL), wsel)],
        out_specs=pl.BlockSpec((EXPERT_ROWS * TILE_ROWS, LANES), out_blk),
        scratch_shapes=[pltpu.VMEM((D_MODEL, D_EXPERT), BF16),
                        pltpu.VMEM((D_MODEL, D_EXPERT), BF16),
                        pltpu.VMEM((D_EXPERT, D_MODEL), BF16)],
    )
    return pl.pallas_call(
        _expert_kernel,
        grid_spec=grid_spec,
        out_shape=jax.ShapeDtypeStruct((N_EXPERT_BLOCKS * EXPERT_ROWS * TILE_ROWS, LANES), F32),
        compiler_params=_params(1),
        name="experts",
    )(block_expert, n_used, xs, w_gate, w_up, w_down)


def _combine_kernel(dest_ref, dest_next_ref, x1_ref, gate_ref, lng_ref, lnb_ref, ys_hbm,
                    x2_ref, buf_ref, sems):
    step = pl.program_id(0)
    n_steps = pl.num_programs(0)
    slot = step % 2

    def issue(d_ref, slot_):
        def body(t, c):
            for k in range(TOP_K):
                pltpu.make_async_copy(_token_tile(ys_hbm, d_ref[TOP_K * t + k]),
                                      _token_tile(buf_ref.at[slot_, k], t),
                                      sems.at[slot_]).start(priority=k % 2)
            return c
        lax.fori_loop(0, GATHER_ROWS, body, 0, unroll=8)

    @pl.when(step == 0)
    def _():
        issue(dest_ref, 0)

    @pl.when(step + 1 < n_steps)
    def _():
        issue(dest_next_ref, 1 - slot)

    def drain(t, c):
        for k in range(TOP_K):
            pltpu.make_async_copy(_token_tile(ys_hbm, 0), _token_tile(buf_ref.at[slot, k], 0),
                                  sems.at[slot]).wait()
        return c

    lax.fori_loop(0, GATHER_ROWS, drain, 0, unroll=8)
    g = gate_ref[...]
    y = (_load_token_tiles(buf_ref.at[slot, 0], GATHER_ROWS) * g[:, 0:1]
         + _load_token_tiles(buf_ref.at[slot, 1], GATHER_ROWS) * g[:, 1:2])
    x1 = _load_token_tiles(x1_ref, GATHER_ROWS)
    x2_ref[...] = _layer_norm(DEEPNORM_ALPHA * x1 + y, lng_ref[...], lnb_ref[...])


def _combine(dest, x1, gate_cols, ln_g, ln_b, ys):
    n_steps = TOKENS // GATHER_ROWS
    row = lambda i: (i, 0)
    fixed = lambda i: (0, 0)
    return pl.pallas_call(
        _combine_kernel,
        grid=(n_steps,),
        in_specs=[pl.BlockSpec((TOP_K * GATHER_ROWS,), lambda i: (i,), memory_space=pltpu.SMEM),
                  pl.BlockSpec((TOP_K * GATHER_ROWS,), lambda i: (jnp.minimum(i + 1, n_steps - 1),),
                               memory_space=pltpu.SMEM),
                  pl.BlockSpec((GATHER_ROWS * TILE_ROWS, LANES), row),
                  pl.BlockSpec((GATHER_ROWS, TOP_K), row),
                  pl.BlockSpec((1, D_MODEL), fixed),
                  pl.BlockSpec((1, D_MODEL), fixed),
                  pl.BlockSpec(memory_space=pl.ANY)],
        out_specs=pl.BlockSpec((GATHER_ROWS, D_MODEL), row),
        out_shape=jax.ShapeDtypeStruct((TOKENS, D_MODEL), F32),
        scratch_shapes=[pltpu.VMEM((2, TOP_K, GATHER_ROWS * TILE_ROWS, LANES), F32),
                        pltpu.SemaphoreType.DMA((2,))],
        compiler_params=_params(1),
        name="combine_ln",
    )(dest, dest, x1, gate_cols, ln_g, ln_b, ys)


def _prepare_layer(l, w_in, b_in, w_branch_a, w_branch_b, w_out, ln1_g, ln1_b):
    def dup_heads(lo):
        h = jnp.arange(HEAD_DIM)
        return jnp.concatenate([lo + kv * HEAD_DIM + h for kv in range(SWA_KV_HEADS) for _ in range(2)])

    scale = HEAD_DIM ** -0.5
    q_a = jnp.arange(0, 512)
    k_a = dup_heads(512)
    v_a = dup_heads(640)
    q_s = jnp.arange(768, 1280)
    kv_s = jnp.arange(1280, 2304)
    cols = jnp.concatenate([q_a, k_a, v_a, q_s, kv_s])
    col_scale = jnp.concatenate([jnp.full((512,), scale), jnp.ones((512,)),
                                 jnp.full((512,), scale), jnp.ones((1024,))]).astype(F32)
    w, b = w_in[l], b_in[l]
    return {
        "w_qkv": (w[:, cols] * col_scale).astype(BF16),
        "b_qkv": (b[cols] * col_scale)[None, :],
        "w_g": w[:, 2304:].astype(BF16),
        "b_g": b[None, 2304:],
        "w_a": w_branch_a[l].astype(BF16),
        "w_b": w_branch_b[l].astype(BF16),
        "w_o": w_out[l].astype(BF16),
        "ln1_g": ln1_g[l][None, :],
        "ln1_b": ln1_b[l][None, :],
    }


def _prepare_shared(w_router, router_bias):
    r = jnp.arange(N_EXPERTS)
    perm = (r % N_GROUPS) * EXPERTS_PER_GROUP + r // N_GROUPS
    wr_t = w_router.T[perm].astype(F32)
    wr_hi = wr_t.astype(BF16)
    wr_lo = (wr_t - wr_hi.astype(F32)).astype(BF16)
    rbias = jnp.broadcast_to(router_bias.astype(F32)[perm][:, None], (N_EXPERTS, LANES))
    t = jnp.arange(ROW_BLOCK)
    strict_upper = (t[:, None] < t[None, :]).astype(BF16)
    upper = jnp.concatenate([strict_upper, jnp.ones((ROW_BLOCK, LANES), BF16)], axis=1)
    k = jnp.arange(SB_KEY_TILE)
    ntri = -(k[:, None] >= k[None, :]).astype(BF16)
    return {"wr_hi": wr_hi, "wr_lo": wr_lo, "rbias": rbias, "upper": upper, "ntri": ntri}


def _block_plan(counts, eid, rank):
    n_blocks = (counts + EXPERT_ROWS - 1) // EXPERT_ROWS
    block_end = jnp.cumsum(n_blocks)
    row_start = (block_end - n_blocks) * EXPERT_ROWS
    n_used = block_end[-1:].astype(jnp.int32)
    blocks = jnp.arange(N_EXPERT_BLOCKS)
    block_expert = jnp.minimum(jnp.sum(block_end[None, :] <= blocks[:, None], axis=1),
                               N_EXPERTS - 1).astype(jnp.int32)
    experts = jnp.arange(N_EXPERTS)
    start_of = jnp.sum(jnp.where(eid[:, :, None] == experts, row_start, 0), axis=-1)
    dest = (start_of + rank).astype(jnp.int32)
    return block_expert, n_used, dest.T.reshape(-1)


def kernel(x, w_in, b_in, attn_sinks, w_branch_a, w_branch_b, w_out, ln1_g, ln1_b, w_router,
           router_bias, w_gate, w_up, w_down, ln2_g, ln2_b):
    shared = _prepare_shared(w_router, router_bias)
    x2 = x.reshape(TOKENS, D_MODEL)
    xs = jnp.zeros((N_EXPERT_BLOCKS * EXPERT_ROWS * TILE_ROWS, LANES), F32)
    for l in range(DEPTH):
        lw = _prepare_layer(l, w_in, b_in, w_branch_a, w_branch_b, w_out, ln1_g, ln1_b)
        qa, ka, va, qs, ks, vs = _inproj(x2, lw["w_qkv"], lw["b_qkv"])
        ya, yb = _attention(qa, ka, va, attn_sinks[l], qs, ks, vs, shared["ntri"])
        x1, eid, gate, rank, cnt = _mix(x2, ya, yb, lw, shared)
        block_expert, n_used, dest = _block_plan(cnt[:, 0].astype(jnp.int32), eid, rank)
        xs = _dispatch(dest, x1, xs)
        ys = _experts(l, block_expert, n_used, xs, w_gate, w_up, w_down)
        x2 = _combine(dest, x1, gate.T, ln2_g[l][None, :], ln2_b[l][None, :], ys)
    return x2.reshape(BATCH, SEQ, D_MODEL)
```

```python
import functools

import jax
import jax.numpy as jnp
from jax import lax
from jax.experimental import pallas as pl
from jax.experimental.pallas import tpu as pltpu

D_MODEL = 1024
BATCH = 16
SEQ = 2048
DEPTH = 4
HEAD_DIM = 64
SWA_Q_HEADS = 8
SWA_KV_HEADS = 2
SWA_GROUP = SWA_Q_HEADS // SWA_KV_HEADS
SB_HEADS = 8
assert SWA_Q_HEADS == SB_HEADS
BLOCK = 128
N_EXPERTS = 32
N_GROUPS = 8
EXPERTS_PER_GROUP = N_EXPERTS // N_GROUPS
TOP_K = 2
D_EXPERT = 512
LN_EPS = 1e-5
DEEPNORM_ALPHA = (2 * DEPTH) ** 0.25

TOKENS = BATCH * SEQ
N_QBLOCKS = SEQ // BLOCK
LANES = 128
SUBLANES = 8
TILE_ROWS = D_MODEL // LANES
assert TILE_ROWS == SUBLANES
D_SWA_Q = SWA_Q_HEADS * HEAD_DIM
D_SWA_KV_DUP = 2 * SWA_KV_HEADS * HEAD_DIM
D_SB = SB_HEADS * HEAD_DIM
SB_LANE_GROUPS = D_SB // LANES
QKV_COLS = D_SWA_Q + 2 * D_SWA_KV_DUP + 3 * D_SB

ROW_BLOCK = 512
SB_KEY_TILE = 256
SB_PIPE_LAG = 2
SB_DEAD_LOG = -110.0
EXPERT_ROWS = 512
N_EXPERT_BLOCKS = (TOKENS * TOP_K) // EXPERT_ROWS + N_EXPERTS
DISPATCH_ROWS = 256
GATHER_ROWS = 256
VMEM_LIMIT = 56 * 1024 * 1024
LOG2E = 1.4426950408889634

BF16 = jnp.bfloat16
F32 = jnp.float32
NT_DIMS = (((1,), (1,)), ((), ()))


def _params(n_axes, vmem=VMEM_LIMIT):
    return pltpu.CompilerParams(dimension_semantics=("arbitrary",) * n_axes,
                                vmem_limit_bytes=vmem)


def _layer_norm(h, g, b):
    mu = jnp.mean(h, axis=-1, keepdims=True)
    c = h - mu
    var = jnp.mean(c * c, axis=-1, keepdims=True)
    return c * lax.rsqrt(var + LN_EPS) * g + b


def _load_token_tiles(ref, n_tokens):
    return jnp.concatenate(
        [ref[pl.ds(j, n_tokens, stride=TILE_ROWS), :] for j in range(TILE_ROWS)], axis=1)


def _store_token_tiles(ref, value, n_tokens):
    for j in range(TILE_ROWS):
        ref[pl.ds(j, n_tokens, stride=TILE_ROWS), :] = value[:, j * LANES:(j + 1) * LANES]


_QKV_SLICES = ((0, 512), (512, 768), (768, 1024), (1024, 1536), (1536, 2048), (2048, 2560))


def _inproj_kernel(x_ref, w_ref, b_ref, *out_refs):
    xb = x_ref[...].astype(BF16)
    for (lo, hi), o_ref in zip(_QKV_SLICES, out_refs):
        p = jnp.dot(xb, w_ref[:, lo:hi], preferred_element_type=F32) + b_ref[:, lo:hi]
        o_ref[...] = p.astype(BF16)


def _inproj(x2, w_qkv, b_qkv):
    widths = [hi - lo for lo, hi in _QKV_SLICES]
    return pl.pallas_call(
        _inproj_kernel,
        grid=(TOKENS // ROW_BLOCK,),
        in_specs=[
            pl.BlockSpec((ROW_BLOCK, D_MODEL), lambda i: (i, 0)),
            pl.BlockSpec((D_MODEL, QKV_COLS), lambda i: (0, 0)),
            pl.BlockSpec((1, QKV_COLS), lambda i: (0, 0)),
        ],
        out_specs=[pl.BlockSpec((ROW_BLOCK, w), lambda i: (i, 0)) for w in widths],
        out_shape=[jax.ShapeDtypeStruct((TOKENS, w), BF16) for w in widths],
        compiler_params=_params(1),
        name="inproj",
    )(x2, w_qkv, b_qkv)


def _swa_stages(n, low_half, sink_ref, q_ref, kp_ref, kc_ref, vp_ref, vc_ref, o_ref):
    qi = lax.broadcasted_iota(jnp.int32, (BLOCK, 2 * BLOCK), 0)
    kj = lax.broadcasted_iota(jnp.int32, (BLOCK, 2 * BLOCK), 1)
    dist = qi + BLOCK - kj
    valid = (dist >= 0) & (dist < BLOCK) & ((kj >= BLOCK) | (n > 0))
    dist_f = dist.astype(F32)
    zero = jnp.zeros((), BF16)
    scores, outs = [], []

    def window(ref_prev, ref_cur, kvh):
        return jnp.concatenate([ref_prev[:, kvh * LANES:(kvh + 1) * LANES],
                                ref_cur[:, kvh * LANES:(kvh + 1) * LANES]], axis=0)

    def start():
        for head in range(SWA_Q_HEADS):
            grp = head // 2
            q = q_ref[:, grp * LANES:(grp + 1) * LANES]
            qm = jnp.where(low_half, q, zero) if head % 2 == 0 else jnp.where(low_half, zero, q)
            scores.append(lax.dot_general(qm, window(kp_ref, kc_ref, head // SWA_GROUP), NT_DIMS,
                                          preferred_element_type=F32))

    def head_step(head):
        slope = 2.0 ** (-8.0 * (head + 1) / SWA_Q_HEADS)
        sink = sink_ref[head]
        s = jnp.where(valid, scores[head] - slope * dist_f, -jnp.inf)
        m = jnp.maximum(jnp.max(s, axis=-1, keepdims=True), sink)
        p = jnp.exp(s - m)
        denom = jnp.sum(p, axis=-1, keepdims=True) + jnp.exp(sink - m)
        pv = jnp.dot(p.astype(BF16), window(vp_ref, vc_ref, head // SWA_GROUP),
                     preferred_element_type=F32)
        outs.append(pv / denom)

    def finish():
        for grp in range(D_SWA_Q // LANES):
            o_ref[:, grp * LANES:(grp + 1) * LANES] = jnp.where(
                low_half, outs[2 * grp], outs[2 * grp + 1]).astype(BF16)

    return start, head_step, finish


def _sb_block(i, low_half, swa_stages, q_ref, k_ref, v_ref, ntri_ref, o_ref, qm_ref, acc_ref,
              carry_ref, z_ref):
    zero = jnp.zeros((), BF16)
    for h in range(SB_HEADS):
        q = q_ref[:, (h // 2) * LANES:(h // 2 + 1) * LANES]
        qm_ref[h] = jnp.where(low_half, q, zero) if h % 2 == 0 else jnp.where(low_half, zero, q)
    acc_ref[...] = jnp.zeros_like(acc_ref)
    carry_ref[...] = jnp.zeros_like(carry_ref)
    sign_bit = jnp.uint32(0x80000000)
    odd = i % 2
    full_end = (i - odd) * BLOCK
    n_full = i // 2

    def scores(h, start, width):
        g = h // 2
        k = k_ref[pl.ds(start, width), g * LANES:(g + 1) * LANES]
        return lax.dot_general(qm_ref[h], k, NT_DIMS, preferred_element_type=F32)

    def sweep(load_z, start, width, tiles_done, strict=None, side_step=None):
        next_start = pl.multiple_of(
            jnp.maximum(full_end - (tiles_done + 1) * SB_KEY_TILE, 0), BLOCK)
        ntri = ntri_ref[:width, :width]
        suffixes = {}
        for step in range(SB_HEADS + SB_PIPE_LAG):
            if step < SB_HEADS and side_step is not None:
                side_step(step)
            if step < SB_HEADS:
                z = load_z(step)
                neg_abs = lax.bitcast_convert_type(
                    lax.bitcast_convert_type(z, jnp.uint32) | sign_bit, F32)
                sp = jnp.maximum(z, 0.0) + jnp.log(1.0 + jnp.exp2(neg_abs * LOG2E))
                if strict is not None:
                    sp = jnp.where(strict, sp, 0.0)
                suffixes[step] = jnp.dot(sp.astype(BF16), ntri, preferred_element_type=F32)
            h = step - SB_PIPE_LAG
            if h >= 0:
                suffix = suffixes.pop(h)
                carry = carry_ref[h]
                carry_w = jnp.concatenate([carry] * (width // LANES), axis=1)
                a = jnp.exp2((load_z(h) + suffix + carry_w) * LOG2E)
                if strict is not None:
                    a = jnp.where(strict, a, 0.0)
                g = h // 2
                v = v_ref[pl.ds(start, width), g * LANES:(g + 1) * LANES]
                acc_ref[h] += jnp.dot(a.astype(BF16), v, preferred_element_type=F32)
                carry_ref[h] = carry + jnp.broadcast_to(suffix[:, 0:1], (BLOCK, LANES))
                z_ref[h] = scores(h, next_start, SB_KEY_TILE)

    @pl.when(odd == 0)
    def _():
        start = pl.multiple_of(i * BLOCK, BLOCK)
        qpos = lax.broadcasted_iota(jnp.int32, (BLOCK, BLOCK), 0)
        kpos = lax.broadcasted_iota(jnp.int32, (BLOCK, BLOCK), 1)
        swa_start, swa_head, swa_finish = swa_stages()
        swa_start()
        diag = [scores(h, start, BLOCK) for h in range(SB_HEADS)]
        sweep(lambda h: diag[h], start, BLOCK, 0, kpos < qpos, swa_head)
        swa_finish()

    @pl.when(odd == 1)
    def _():
        start = pl.multiple_of((i - 1) * BLOCK, BLOCK)
        qpos = lax.broadcasted_iota(jnp.int32, (BLOCK, SB_KEY_TILE), 0) + BLOCK
        kpos = lax.broadcasted_iota(jnp.int32, (BLOCK, SB_KEY_TILE), 1)
        swa_start, swa_head, swa_finish = swa_stages()
        swa_start()
        for h in range(SB_HEADS):
            z_ref[h] = scores(h, start, SB_KEY_TILE)
        sweep(lambda h: z_ref[h], start, SB_KEY_TILE, 0, kpos < qpos, swa_head)
        swa_finish()

    def body(state):
        j, _ = state
        start = pl.multiple_of(full_end - (j + 1) * SB_KEY_TILE, BLOCK)
        sweep(lambda h: z_ref[h], start, SB_KEY_TILE, j + 1)
        return j + 1, (jnp.max(carry_ref[...]) > SB_DEAD_LOG).astype(jnp.int32)

    lax.while_loop(lambda state: (state[0] < n_full) & (state[1] > 0), body,
                   (jnp.int32(0), jnp.int32(1)))

    for g in range(SB_LANE_GROUPS):
        o_ref[:, g * LANES:(g + 1) * LANES] = jnp.where(
            low_half, acc_ref[2 * g], acc_ref[2 * g + 1]).astype(BF16)


def _attn_kernel(sink_ref, qa_ref, ka_ref, va_ref, qs_ref, ks_ref, vs_ref, ntri_ref,
                 ya_ref, yb_ref, qm_ref, acc_ref, carry_ref, z_ref):
    lane = lax.broadcasted_iota(jnp.int32, (BLOCK, LANES), 1)
    low_half = lane < HEAD_DIM

    def query_block(i, c):
        rows = pl.ds(pl.multiple_of(i * BLOCK, BLOCK), BLOCK)
        prev = pl.ds(pl.multiple_of(jnp.maximum(i - 1, 0) * BLOCK, BLOCK), BLOCK)
        swa_stages = functools.partial(
            _swa_stages, i, low_half, sink_ref, qa_ref.at[rows], ka_ref.at[prev], ka_ref.at[rows],
            va_ref.at[prev], va_ref.at[rows], ya_ref.at[rows])
        _sb_block(i, low_half, swa_stages, qs_ref.at[rows], ks_ref, vs_ref, ntri_ref,
                  yb_ref.at[rows], qm_ref, acc_ref, carry_ref, z_ref)
        return c

    lax.fori_loop(0, N_QBLOCKS, query_block, 0)


def _attention(qa, ka, va, sinks, qs, ks, vs, ntri):
    seq = lambda b: (b, 0)
    return pl.pallas_call(
        _attn_kernel,
        grid=(BATCH,),
        in_specs=[
            pl.BlockSpec(memory_space=pltpu.SMEM),
            pl.BlockSpec((SEQ, D_SWA_Q), seq),
            pl.BlockSpec((SEQ, D_SWA_KV_DUP), seq),
            pl.BlockSpec((SEQ, D_SWA_KV_DUP), seq),
            pl.BlockSpec((SEQ, D_SB), seq),
            pl.BlockSpec((SEQ, D_SB), seq),
            pl.BlockSpec((SEQ, D_SB), seq),
            pl.BlockSpec((SB_KEY_TILE, SB_KEY_TILE), lambda b: (0, 0)),
        ],
        out_specs=[pl.BlockSpec((SEQ, D_SWA_Q), seq), pl.BlockSpec((SEQ, D_SB), seq)],
        out_shape=[jax.ShapeDtypeStruct((TOKENS, D_SWA_Q), BF16),
                   jax.ShapeDtypeStruct((TOKENS, D_SB), BF16)],
        scratch_shapes=[pltpu.VMEM((SB_HEADS, BLOCK, LANES), BF16),
                        pltpu.VMEM((SB_HEADS, BLOCK, LANES), F32),
                        pltpu.VMEM((SB_HEADS, BLOCK, LANES), F32),
                        pltpu.VMEM((SB_HEADS, BLOCK, SB_KEY_TILE), F32)],
        compiler_params=_params(1),
        name="attention",
    )(sinks, qa, ka, va, qs, ks, vs, ntri)


def _max4(a, b, c, d):
    return jnp.maximum(jnp.maximum(a, b), jnp.maximum(c, d))


def _first_index_of(vals, target):
    idx = jnp.full(target.shape, len(vals) - 1, jnp.int32)
    for p in range(len(vals) - 2, -1, -1):
        idx = jnp.where(vals[p] == target, p, idx)
    return idx


def _select_by_index(vals, idx):
    out = vals[-1]
    for p in range(len(vals) - 2, -1, -1):
        out = jnp.where(idx == p, vals[p], out)
    return out


def _mix_kernel(x_ref, ya_ref, yb_ref, wg_ref, bg_ref, wa_ref, wb_ref, wo_ref, lng_ref, lnb_ref,
                wrh_ref, wrl_ref, rb_ref, upper_ref,
                x1_ref, eid_ref, gate_ref, rank_ref, cnt_ref, base_ref):
    step = pl.program_id(0)
    x = x_ref[...]
    gates = jnp.dot(x.astype(BF16), wg_ref[...], preferred_element_type=F32) + bg_ref[...]
    pa = jnp.dot(ya_ref[...], wa_ref[...], preferred_element_type=F32)
    pb = jnp.dot(yb_ref[...], wb_ref[...], preferred_element_type=F32)
    merged = (jax.nn.sigmoid(gates[:, :D_MODEL]) * pa + jax.nn.sigmoid(gates[:, D_MODEL:]) * pb)
    mix = jnp.dot(merged.astype(BF16), wo_ref[...], preferred_element_type=F32)
    x1 = _layer_norm(DEEPNORM_ALPHA * x + mix, lng_ref[...], lnb_ref[...])
    _store_token_tiles(x1_ref, x1, ROW_BLOCK)

    x1h = x1.astype(BF16)
    x1l = (x1 - x1h.astype(F32)).astype(BF16)
    wrh = wrh_ref[...]
    logits = (lax.dot_general(wrh, x1h, NT_DIMS, preferred_element_type=F32)
              + lax.dot_general(wrh, x1l, NT_DIMS, preferred_element_type=F32)
              + lax.dot_general(wrl_ref[...], x1h, NT_DIMS, preferred_element_type=F32))
    aff = jax.nn.sigmoid(logits)
    reps = ROW_BLOCK // LANES
    biased = aff + jnp.concatenate([rb_ref[...]] * reps, axis=1)
    bm = [biased[p * N_GROUPS:(p + 1) * N_GROUPS, :] for p in range(EXPERTS_PER_GROUP)]
    am = [aff[p * N_GROUPS:(p + 1) * N_GROUPS, :] for p in range(EXPERTS_PER_GROUP)]
    hi1, lo1 = jnp.maximum(bm[0], bm[1]), jnp.minimum(bm[0], bm[1])
    hi2, lo2 = jnp.maximum(bm[2], bm[3]), jnp.minimum(bm[2], bm[3])
    group_score = (jnp.maximum(hi1, hi2)
                   + jnp.maximum(jnp.minimum(hi1, hi2), jnp.maximum(lo1, lo2)))
    group_iota = lax.broadcasted_iota(jnp.int32, (N_GROUPS, ROW_BLOCK), 0)
    best = jnp.max(group_score, axis=0, keepdims=True)
    g_sel = jnp.min(jnp.where(group_score == best, group_iota, N_GROUPS), axis=0, keepdims=True)
    in_sel = group_iota == g_sel
    bsel = [jnp.sum(jnp.where(in_sel, b, 0.0), axis=0, keepdims=True) for b in bm]
    asel = [jnp.sum(jnp.where(in_sel, a, 0.0), axis=0, keepdims=True) for a in am]
    i1 = _first_index_of(bsel, _max4(*bsel))
    rest = [jnp.where(i1 == p, -jnp.inf, bsel[p]) for p in range(EXPERTS_PER_GROUP)]
    i2 = _first_index_of(rest, _max4(*rest))
    a1 = _select_by_index(asel, i1)
    a2 = _select_by_index(asel, i2)
    e1 = g_sel * EXPERTS_PER_GROUP + i1
    e2 = g_sel * EXPERTS_PER_GROUP + i2
    eid_ref[...] = jnp.concatenate([e1, e2], axis=0)
    gate_ref[...] = jnp.concatenate([a1 / (a1 + a2), a2 / (a1 + a2)], axis=0)

    @pl.when(step == 0)
    def _():
        base_ref[...] = jnp.zeros_like(base_ref)

    expert_iota = lax.broadcasted_iota(jnp.int32, (N_EXPERTS, ROW_BLOCK), 0)
    oh1 = expert_iota == e1
    oh2 = expert_iota == e2
    upper = upper_ref[...]
    c1 = jnp.dot(jnp.where(oh1, 1.0, 0.0).astype(BF16), upper, preferred_element_type=F32)
    c2 = jnp.dot(jnp.where(oh2, 1.0, 0.0).astype(BF16), upper, preferred_element_type=F32)
    base = base_ref[...]
    tot1 = c1[:, ROW_BLOCK:]
    tot2 = c2[:, ROW_BLOCK:]
    before1 = jnp.concatenate([base] * reps, axis=1) + c1[:, :ROW_BLOCK]
    before2 = jnp.concatenate([base + tot1] * reps, axis=1) + c2[:, :ROW_BLOCK]
    r1 = jnp.sum(jnp.where(oh1, before1, 0.0), axis=0, keepdims=True)
    r2 = jnp.sum(jnp.where(oh2, before2, 0.0), axis=0, keepdims=True)
    rank_ref[...] = jnp.concatenate([r1, r2], axis=0).astype(jnp.int32)
    new_base = base + tot1 + tot2
    base_ref[...] = new_base
    cnt_ref[...] = new_base


def _mix(x2, ya, yb, lw, shared):
    row = lambda i: (i, 0)
    fixed = lambda i: (0, 0)
    col = lambda i: (0, i)
    full = lambda a: pl.BlockSpec(a.shape, fixed)
    weights = [lw["w_g"], lw["b_g"], lw["w_a"], lw["w_b"], lw["w_o"], lw["ln1_g"], lw["ln1_b"],
               shared["wr_hi"], shared["wr_lo"], shared["rbias"], shared["upper"]]
    return pl.pallas_call(
        _mix_kernel,
        grid=(TOKENS // ROW_BLOCK,),
        in_specs=[pl.BlockSpec((ROW_BLOCK, D_MODEL), row),
                  pl.BlockSpec((ROW_BLOCK, D_SWA_Q), row),
                  pl.BlockSpec((ROW_BLOCK, D_SB), row)] + [full(w) for w in weights],
        out_specs=[pl.BlockSpec((ROW_BLOCK * TILE_ROWS, LANES), row),
                   pl.BlockSpec((TOP_K, ROW_BLOCK), col),
                   pl.BlockSpec((TOP_K, ROW_BLOCK), col),
                   pl.BlockSpec((TOP_K, ROW_BLOCK), col),
                   pl.BlockSpec((N_EXPERTS, LANES), fixed)],
        out_shape=[jax.ShapeDtypeStruct((TOKENS * TILE_ROWS, LANES), F32),
                   jax.ShapeDtypeStruct((TOP_K, TOKENS), jnp.int32),
                   jax.ShapeDtypeStruct((TOP_K, TOKENS), F32),
                   jax.ShapeDtypeStruct((TOP_K, TOKENS), jnp.int32),
                   jax.ShapeDtypeStruct((N_EXPERTS, LANES), F32)],
        scratch_shapes=[pltpu.VMEM((N_EXPERTS, LANES), F32)],
        compiler_params=_params(1),
        name="mix_ln_route",
    )(x2, ya, yb, *weights)


def _token_tile(ref, row):
    return ref.at[pl.ds(pl.multiple_of(row * TILE_ROWS, TILE_ROWS), TILE_ROWS)]


def _dispatch_kernel(dest_ref, x1_ref, xs_in_hbm, xs_hbm, stage_ref, sems):
    del xs_in_hbm
    step = pl.program_id(0)
    n_steps = pl.num_programs(0)
    slot = step % 2
    stage_ref[slot] = x1_ref[...]

    def issue(t, c):
        for k in range(TOP_K):
            pltpu.make_async_copy(_token_tile(stage_ref.at[slot], t),
                                  _token_tile(xs_hbm, dest_ref[TOP_K * t + k]),
                                  sems.at[slot]).start(priority=k % 2)
        return c

    lax.fori_loop(0, DISPATCH_ROWS, issue, 0, unroll=8)

    def drain(slot_):
        def body(t, c):
            for k in range(TOP_K):
                pltpu.make_async_copy(_token_tile(stage_ref.at[slot_], 0), _token_tile(xs_hbm, 0),
                                      sems.at[slot_]).wait()
            return c
        lax.fori_loop(0, DISPATCH_ROWS, body, 0, unroll=8)

    @pl.when(step > 0)
    def _():
        drain(1 - slot)

    @pl.when(step == n_steps - 1)
    def _():
        drain(slot)


def _dispatch(dest, x1, xs):
    return pl.pallas_call(
        _dispatch_kernel,
        grid=(TOKENS // DISPATCH_ROWS,),
        in_specs=[pl.BlockSpec((TOP_K * DISPATCH_ROWS,), lambda i: (i,), memory_space=pltpu.SMEM),
                  pl.BlockSpec((DISPATCH_ROWS * TILE_ROWS, LANES), lambda i: (i, 0)),
                  pl.BlockSpec(memory_space=pl.ANY)],
        out_specs=pl.BlockSpec(memory_space=pl.ANY),
        out_shape=jax.ShapeDtypeStruct(xs.shape, xs.dtype),
        scratch_shapes=[pltpu.VMEM((2, DISPATCH_ROWS * TILE_ROWS, LANES), F32),
                        pltpu.SemaphoreType.DMA((2,))],
        input_output_aliases={2: 0},
        compiler_params=_params(1),
        name="dispatch",
    )(dest, x1, xs)


def _expert_kernel(be_ref, nu_ref, x_ref, wg_ref, wu_ref, wd_ref, y_ref, wg_bf, wu_bf, wd_bf):
    i = pl.program_id(0)

    @pl.when(i < nu_ref[0])
    def _():
        e = be_ref[i]
        e_prev = be_ref[jnp.maximum(i - 1, 0)]

        @pl.when((i == 0) | (e != e_prev))
        def _():
            wg_bf[...] = wg_ref[...].astype(BF16)
            wu_bf[...] = wu_ref[...].astype(BF16)
            wd_bf[...] = wd_ref[...].astype(BF16)

        xb = _load_token_tiles(x_ref, EXPERT_ROWS).astype(BF16)
        hg = jnp.dot(xb, wg_bf[...], preferred_element_type=F32)
        hu = jnp.dot(xb, wu_bf[...], preferred_element_type=F32)
        h = hg * jax.nn.sigmoid(hg) * hu
        y = jnp.dot(h.astype(BF16), wd_bf[...], preferred_element_type=F32)
        _store_token_tiles(y_ref, y, EXPERT_ROWS)

    @pl.when(i >= nu_ref[0])
    def _():
        y_ref[...] = jnp.zeros_like(y_ref)


def _experts(layer, block_expert, n_used, xs, w_gate, w_up, w_down):
    blk = lambda i, be, nu: (jnp.minimum(i, nu[0] - 1), 0)
    out_blk = lambda i, be, nu: (i, 0)
    wsel = lambda i, be, nu: (layer, be[jnp.minimum(i, nu[0] - 1)], 0, 0)
    grid_spec = pltpu.PrefetchScalarGridSpec(
        num_scalar_prefetch=2,
        grid=(N_EXPERT_BLOCKS,),
        in_specs=[pl.BlockSpec((EXPERT_ROWS * TILE_ROWS, LANES), blk),
                  pl.BlockSpec((None, None, D_MODEL, D_EXPERT), wsel),
                  pl.BlockSpec((None, None, D_MODEL, D_EXPERT), wsel),
                  pl.BlockSpec((None, None, D_EXPERT, D_MODEL), wsel)],
        out_specs=pl.BlockSpec((EXPERT_ROWS * TILE_ROWS, LANES), out_blk),
        scratch_shapes=[pltpu.VMEM((D_MODEL, D_EXPERT), BF16),
                        pltpu.VMEM((D_MODEL, D_EXPERT), BF16),
                        pltpu.VMEM((D_EXPERT, D_MODEL), BF16)],
    )
    return pl.pallas_call(
        _expert_kernel,
        grid_spec=grid_spec,
        out_shape=jax.ShapeDtypeStruct((N_EXPERT_BLOCKS * EXPERT_ROWS * TILE_ROWS, LANES), F32),
        compiler_params=_params(1),
        name="experts",
    )(block_expert, n_used, xs, w_gate, w_up, w_down)


def _combine_kernel(dest_ref, dest_next_ref, x1_ref, gate_ref, lng_ref, lnb_ref, ys_hbm,
                    x2_ref, buf_ref, sems):
    step = pl.program_id(0)
    n_steps = pl.num_programs(0)
    slot = step % 2

    def issue(d_ref, slot_):
        def body(t, c):
            for k in range(TOP_K):
                pltpu.make_async_copy(_token_tile(ys_hbm, d_ref[TOP_K * t + k]),
                                      _token_tile(buf_ref.at[slot_, k], t),
                                      sems.at[slot_]).start(priority=k % 2)
            return c
        lax.fori_loop(0, GATHER_ROWS, body, 0, unroll=8)

    @pl.when(step == 0)
    def _():
        issue(dest_ref, 0)

    @pl.when(step + 1 < n_steps)
    def _():
        issue(dest_next_ref, 1 - slot)

    def drain(t, c):
        for k in range(TOP_K):
            pltpu.make_async_copy(_token_tile(ys_hbm, 0), _token_tile(buf_ref.at[slot, k], 0),
                                  sems.at[slot]).wait()
        return c

    lax.fori_loop(0, GATHER_ROWS, drain, 0, unroll=8)
    g = gate_ref[...]
    y = (_load_token_tiles(buf_ref.at[slot, 0], GATHER_ROWS) * g[:, 0:1]
         + _load_token_tiles(buf_ref.at[slot, 1], GATHER_ROWS) * g[:, 1:2])
    x1 = _load_token_tiles(x1_ref, GATHER_ROWS)
    x2_ref[...] = _layer_norm(DEEPNORM_ALPHA * x1 + y, lng_ref[...], lnb_ref[...])


def _combine(dest, x1, gate_cols, ln_g, ln_b, ys):
    n_steps = TOKENS // GATHER_ROWS
    row = lambda i: (i, 0)
    fixed = lambda i: (0, 0)
    return pl.pallas_call(
        _combine_kernel,
        grid=(n_steps,),
        in_specs=[pl.BlockSpec((TOP_K * GATHER_ROWS,), lambda i: (i,), memory_space=pltpu.SMEM),
                  pl.BlockSpec((TOP_K * GATHER_ROWS,), lambda i: (jnp.minimum(i + 1, n_steps - 1),),
                               memory_space=pltpu.SMEM),
                  pl.BlockSpec((GATHER_ROWS * TILE_ROWS, LANES), row),
                  pl.BlockSpec((GATHER_ROWS, TOP_K), row),
                  pl.BlockSpec((1, D_MODEL), fixed),
                  pl.BlockSpec((1, D_MODEL), fixed),
                  pl.BlockSpec(memory_space=pl.ANY)],
        out_specs=pl.BlockSpec((GATHER_ROWS, D_MODEL), row),
        out_shape=jax.ShapeDtypeStruct((TOKENS, D_MODEL), F32),
        scratch_shapes=[pltpu.VMEM((2, TOP_K, GATHER_ROWS * TILE_ROWS, LANES), F32),
                        pltpu.SemaphoreType.DMA((2,))],
        compiler_params=_params(1),
        name="combine_ln",
    )(dest, dest, x1, gate_cols, ln_g, ln_b, ys)


def _prepare_layer(l, w_in, b_in, w_branch_a, w_branch_b, w_out, ln1_g, ln1_b):
    scale = HEAD_DIM ** -0.5

    def columns(t):
        head = lambda lo, h: t[..., lo + h * HEAD_DIM:lo + (h + 1) * HEAD_DIM]
        dup = lambda lo: [head(lo, h) for h in range(SWA_KV_HEADS) for _ in range(2)]
        return jnp.concatenate([t[..., 0:512] * scale] + dup(512) + dup(640)
                               + [t[..., 768:1280] * scale, t[..., 1280:2304]], axis=-1)

    w, b = w_in[l], b_in[l]
    return {
        "w_qkv": columns(w).astype(BF16),
        "b_qkv": columns(b)[None, :],
        "w_g": w[:, 2304:].astype(BF16),
        "b_g": b[None, 2304:],
        "w_a": w_branch_a[l].astype(BF16),
        "w_b": w_branch_b[l].astype(BF16),
        "w_o": w_out[l].astype(BF16),
        "ln1_g": ln1_g[l][None, :],
        "ln1_b": ln1_b[l][None, :],
    }


def _prepare_shared(w_router, router_bias):
    r = jnp.arange(N_EXPERTS)
    perm = (r % N_GROUPS) * EXPERTS_PER_GROUP + r // N_GROUPS
    wr_t = w_router.T[perm].astype(F32)
    wr_hi = wr_t.astype(BF16)
    wr_lo = (wr_t - wr_hi.astype(F32)).astype(BF16)
    rbias = jnp.broadcast_to(router_bias.astype(F32)[perm][:, None], (N_EXPERTS, LANES))
    t = jnp.arange(ROW_BLOCK)
    strict_upper = (t[:, None] < t[None, :]).astype(BF16)
    upper = jnp.concatenate([strict_upper, jnp.ones((ROW_BLOCK, LANES), BF16)], axis=1)
    k = jnp.arange(SB_KEY_TILE)
    ntri = -(k[:, None] >= k[None, :]).astype(BF16)
    return {"wr_hi": wr_hi, "wr_lo": wr_lo, "rbias": rbias, "upper": upper, "ntri": ntri}


def _block_plan(counts, eid, rank):
    n_blocks = (counts + EXPERT_ROWS - 1) // EXPERT_ROWS
    block_end = jnp.cumsum(n_blocks)
    row_start = (block_end - n_blocks) * EXPERT_ROWS
    n_used = block_end[-1:].astype(jnp.int32)
    blocks = jnp.arange(N_EXPERT_BLOCKS)
    block_expert = jnp.minimum(jnp.sum(block_end[None, :] <= blocks[:, None], axis=1),
                               N_EXPERTS - 1).astype(jnp.int32)
    experts = jnp.arange(N_EXPERTS)
    start_of = jnp.sum(jnp.where(eid[:, :, None] == experts, row_start, 0), axis=-1)
    dest = (start_of + rank).astype(jnp.int32)
    return block_expert, n_used, dest.T.reshape(-1)


def kernel(x, w_in, b_in, attn_sinks, w_branch_a, w_branch_b, w_out, ln1_g, ln1_b, w_router,
           router_bias, w_gate, w_up, w_down, ln2_g, ln2_b):
    shared = _prepare_shared(w_router, router_bias)
    x2 = x.reshape(TOKENS, D_MODEL)
    xs = jnp.zeros((N_EXPERT_BLOCKS * EXPERT_ROWS * TILE_ROWS, LANES), F32)
    for l in range(DEPTH):
        lw = _prepare_layer(l, w_in, b_in, w_branch_a, w_branch_b, w_out, ln1_g, ln1_b)
        qa, ka, va, qs, ks, vs = _inproj(x2, lw["w_qkv"], lw["b_qkv"])
        ya, yb = _attention(qa, ka, va, attn_sinks[l], qs, ks, vs, shared["ntri"])
        x1, eid, gate, rank, cnt = _mix(x2, ya, yb, lw, shared)
        block_expert, n_used, dest = _block_plan(cnt[:, 0].astype(jnp.int32), eid, rank)
        xs = _dispatch(dest, x1, xs)
        ys = _experts(l, block_expert, n_used, xs, w_gate, w_up, w_down)
        x2 = _combine(dest, x1, gate.T, ln2_g[l][None, :], ln2_b[l][None, :], ys)
    return x2.reshape(BATCH, SEQ, D_MODEL)
```

```python
import functools

import jax
import jax.numpy as jnp
from jax import lax
from jax.experimental import pallas as pl
from jax.experimental.pallas import tpu as pltpu

D_MODEL = 1024
BATCH = 16
SEQ = 2048
DEPTH = 4
HEAD_DIM = 64
SWA_Q_HEADS = 8
SWA_KV_HEADS = 2
SWA_GROUP = SWA_Q_HEADS // SWA_KV_HEADS
SB_HEADS = 8
assert SWA_Q_HEADS == SB_HEADS
BLOCK = 128
N_EXPERTS = 32
N_GROUPS = 8
EXPERTS_PER_GROUP = N_EXPERTS // N_GROUPS
TOP_K = 2
D_EXPERT = 512
LN_EPS = 1e-5
DEEPNORM_ALPHA = (2 * DEPTH) ** 0.25

TOKENS = BATCH * SEQ
N_QBLOCKS = SEQ // BLOCK
LANES = 128
SUBLANES = 8
TILE_ROWS = D_MODEL // LANES
assert TILE_ROWS == SUBLANES
D_SWA_Q = SWA_Q_HEADS * HEAD_DIM
D_SWA_KV_DUP = 2 * SWA_KV_HEADS * HEAD_DIM
D_SB = SB_HEADS * HEAD_DIM
SB_LANE_GROUPS = D_SB // LANES
QKV_COLS = D_SWA_Q + 2 * D_SWA_KV_DUP + 3 * D_SB

INPROJ_ROWS = 1024
ROW_BLOCK = 512
SB_KEY_TILE = 256
SB_PIPE_LAG = 2
SB_DEAD_LOG = -110.0
EXPERT_ROWS = 512
N_EXPERT_BLOCKS = (TOKENS * TOP_K) // EXPERT_ROWS + N_EXPERTS
DISPATCH_ROWS = 256
GATHER_ROWS = 256
VMEM_LIMIT = 56 * 1024 * 1024
LOG2E = 1.4426950408889634

BF16 = jnp.bfloat16
F32 = jnp.float32
NT_DIMS = (((1,), (1,)), ((), ()))


def _params(n_axes, vmem=VMEM_LIMIT):
    return pltpu.CompilerParams(dimension_semantics=("arbitrary",) * n_axes,
                                vmem_limit_bytes=vmem)


def _layer_norm(h, g, b):
    mu = jnp.mean(h, axis=-1, keepdims=True)
    c = h - mu
    var = jnp.mean(c * c, axis=-1, keepdims=True)
    return c * lax.rsqrt(var + LN_EPS) * g + b


def _load_token_tiles(ref, n_tokens):
    return jnp.concatenate(
        [ref[pl.ds(j, n_tokens, stride=TILE_ROWS), :] for j in range(TILE_ROWS)], axis=1)


def _store_token_tiles(ref, value, n_tokens):
    for j in range(TILE_ROWS):
        ref[pl.ds(j, n_tokens, stride=TILE_ROWS), :] = value[:, j * LANES:(j + 1) * LANES]


_QKV_SLICES = ((0, 512), (512, 768), (768, 1024), (1024, 1536), (1536, 2048), (2048, 2560))


def _inproj_kernel(x_ref, w_ref, b_ref, *out_refs):
    xb = x_ref[...].astype(BF16)
    for (lo, hi), o_ref in zip(_QKV_SLICES, out_refs):
        p = jnp.dot(xb, w_ref[:, lo:hi], preferred_element_type=F32) + b_ref[:, lo:hi]
        o_ref[...] = p.astype(BF16)


def _inproj(x2, w_qkv, b_qkv):
    widths = [hi - lo for lo, hi in _QKV_SLICES]
    return pl.pallas_call(
        _inproj_kernel,
        grid=(TOKENS // INPROJ_ROWS,),
        in_specs=[
            pl.BlockSpec((INPROJ_ROWS, D_MODEL), lambda i: (i, 0)),
            pl.BlockSpec((D_MODEL, QKV_COLS), lambda i: (0, 0)),
            pl.BlockSpec((1, QKV_COLS), lambda i: (0, 0)),
        ],
        out_specs=[pl.BlockSpec((INPROJ_ROWS, w), lambda i: (i, 0)) for w in widths],
        out_shape=[jax.ShapeDtypeStruct((TOKENS, w), BF16) for w in widths],
        compiler_params=_params(1),
        name="inproj",
    )(x2, w_qkv, b_qkv)


def _swa_stages(n, low_half, sink_ref, q_ref, kp_ref, kc_ref, vp_ref, vc_ref, o_ref):
    qi = lax.broadcasted_iota(jnp.int32, (BLOCK, 2 * BLOCK), 0)
    kj = lax.broadcasted_iota(jnp.int32, (BLOCK, 2 * BLOCK), 1)
    dist = qi + BLOCK - kj
    valid = (dist >= 0) & (dist < BLOCK) & ((kj >= BLOCK) | (n > 0))
    dist_f = dist.astype(F32)
    zero = jnp.zeros((), BF16)
    scores, outs = [], []

    def window(ref_prev, ref_cur, kvh):
        return jnp.concatenate([ref_prev[:, kvh * LANES:(kvh + 1) * LANES],
                                ref_cur[:, kvh * LANES:(kvh + 1) * LANES]], axis=0)

    def start():
        for head in range(SWA_Q_HEADS):
            grp = head // 2
            q = q_ref[:, grp * LANES:(grp + 1) * LANES]
            qm = jnp.where(low_half, q, zero) if head % 2 == 0 else jnp.where(low_half, zero, q)
            scores.append(lax.dot_general(qm, window(kp_ref, kc_ref, head // SWA_GROUP), NT_DIMS,
                                          preferred_element_type=F32))

    def head_step(head):
        slope = 2.0 ** (-8.0 * (head + 1) / SWA_Q_HEADS)
        sink = sink_ref[head]
        s = jnp.where(valid, scores[head] - slope * dist_f, -jnp.inf)
        m = jnp.maximum(jnp.max(s, axis=-1, keepdims=True), sink)
        p = jnp.exp(s - m)
        denom = jnp.sum(p, axis=-1, keepdims=True) + jnp.exp(sink - m)
        pv = jnp.dot(p.astype(BF16), window(vp_ref, vc_ref, head // SWA_GROUP),
                     preferred_element_type=F32)
        outs.append(pv / denom)

    def finish():
        for grp in range(D_SWA_Q // LANES):
            o_ref[:, grp * LANES:(grp + 1) * LANES] = jnp.where(
                low_half, outs[2 * grp], outs[2 * grp + 1]).astype(BF16)

    return start, head_step, finish


def _sb_block(i, low_half, swa_stages, q_ref, k_ref, v_ref, ntri_ref, o_ref, qm_ref, acc_ref,
              carry_ref, z_ref):
    zero = jnp.zeros((), BF16)
    for h in range(SB_HEADS):
        q = q_ref[:, (h // 2) * LANES:(h // 2 + 1) * LANES]
        qm_ref[h] = jnp.where(low_half, q, zero) if h % 2 == 0 else jnp.where(low_half, zero, q)
    acc_ref[...] = jnp.zeros_like(acc_ref)
    carry_ref[...] = jnp.zeros_like(carry_ref)
    sign_bit = jnp.uint32(0x80000000)
    odd = i % 2
    full_end = (i - odd) * BLOCK
    n_full = i // 2

    def scores(h, start, width):
        g = h // 2
        k = k_ref[pl.ds(start, width), g * LANES:(g + 1) * LANES]
        return lax.dot_general(qm_ref[h], k, NT_DIMS, preferred_element_type=F32)

    def sweep(load_z, start, width, tiles_done, strict=None, side_step=None):
        next_start = pl.multiple_of(
            jnp.maximum(full_end - (tiles_done + 1) * SB_KEY_TILE, 0), BLOCK)
        ntri = ntri_ref[:width, :width]
        suffixes = {}
        for step in range(SB_HEADS + SB_PIPE_LAG):
            if step < SB_HEADS and side_step is not None:
                side_step(step)
            if step < SB_HEADS:
                z = load_z(step)
                neg_abs = lax.bitcast_convert_type(
                    lax.bitcast_convert_type(z, jnp.uint32) | sign_bit, F32)
                sp = jnp.maximum(z, 0.0) + jnp.log(1.0 + jnp.exp2(neg_abs * LOG2E))
                if strict is not None:
                    sp = jnp.where(strict, sp, 0.0)
                suffixes[step] = jnp.dot(sp.astype(BF16), ntri, preferred_element_type=F32)
            h = step - SB_PIPE_LAG
            if h >= 0:
                suffix = suffixes.pop(h)
                carry = carry_ref[h]
                carry_w = jnp.concatenate([carry] * (width // LANES), axis=1)
                a = jnp.exp2((load_z(h) + suffix + carry_w) * LOG2E)
                if strict is not None:
                    a = jnp.where(strict, a, 0.0)
                g = h // 2
                v = v_ref[pl.ds(start, width), g * LANES:(g + 1) * LANES]
                acc_ref[h] += jnp.dot(a.astype(BF16), v, preferred_element_type=F32)
                new_carry = carry + jnp.broadcast_to(suffix[:, 0:1], (BLOCK, LANES))
                carry_ref[h] = new_carry
                max_carry = new_carry if h == 0 else jnp.maximum(max_carry, new_carry)
                z_ref[h] = scores(h, next_start, SB_KEY_TILE)
        return max_carry

    @pl.when(odd == 0)
    def _():
        start = pl.multiple_of(i * BLOCK, BLOCK)
        qpos = lax.broadcasted_iota(jnp.int32, (BLOCK, BLOCK), 0)
        kpos = lax.broadcasted_iota(jnp.int32, (BLOCK, BLOCK), 1)
        swa_start, swa_head, swa_finish = swa_stages()
        swa_start()
        diag = [scores(h, start, BLOCK) for h in range(SB_HEADS)]
        sweep(lambda h: diag[h], start, BLOCK, 0, kpos < qpos, swa_head)
        swa_finish()

    @pl.when(odd == 1)
    def _():
        start = pl.multiple_of((i - 1) * BLOCK, BLOCK)
        qpos = lax.broadcasted_iota(jnp.int32, (BLOCK, SB_KEY_TILE), 0) + BLOCK
        kpos = lax.broadcasted_iota(jnp.int32, (BLOCK, SB_KEY_TILE), 1)
        swa_start, swa_head, swa_finish = swa_stages()
        swa_start()
        for h in range(SB_HEADS):
            z_ref[h] = scores(h, start, SB_KEY_TILE)
        sweep(lambda h: z_ref[h], start, SB_KEY_TILE, 0, kpos < qpos, swa_head)
        swa_finish()

    def body(state):
        j, _ = state
        start = pl.multiple_of(full_end - (j + 1) * SB_KEY_TILE, BLOCK)
        max_carry = sweep(lambda h: z_ref[h], start, SB_KEY_TILE, j + 1)
        return j + 1, (jnp.max(max_carry) > SB_DEAD_LOG).astype(jnp.int32)

    lax.while_loop(lambda state: (state[0] < n_full) & (state[1] > 0), body,
                   (jnp.int32(0), jnp.int32(1)))

    for g in range(SB_LANE_GROUPS):
        o_ref[:, g * LANES:(g + 1) * LANES] = jnp.where(
            low_half, acc_ref[2 * g], acc_ref[2 * g + 1]).astype(BF16)


def _attn_kernel(sink_ref, qa_ref, ka_ref, va_ref, qs_ref, ks_ref, vs_ref, ntri_ref,
                 ya_ref, yb_ref, qm_ref, acc_ref, carry_ref, z_ref):
    lane = lax.broadcasted_iota(jnp.int32, (BLOCK, LANES), 1)
    low_half = lane < HEAD_DIM

    def query_block(i, c):
        rows = pl.ds(pl.multiple_of(i * BLOCK, BLOCK), BLOCK)
        prev = pl.ds(pl.multiple_of(jnp.maximum(i - 1, 0) * BLOCK, BLOCK), BLOCK)
        swa_stages = functools.partial(
            _swa_stages, i, low_half, sink_ref, qa_ref.at[rows], ka_ref.at[prev], ka_ref.at[rows],
            va_ref.at[prev], va_ref.at[rows], ya_ref.at[rows])
        _sb_block(i, low_half, swa_stages, qs_ref.at[rows], ks_ref, vs_ref, ntri_ref,
                  yb_ref.at[rows], qm_ref, acc_ref, carry_ref, z_ref)
        return c

    lax.fori_loop(0, N_QBLOCKS, query_block, 0)


def _attention(qa, ka, va, sinks, qs, ks, vs, ntri):
    seq = lambda b: (b, 0)
    return pl.pallas_call(
        _attn_kernel,
        grid=(BATCH,),
        in_specs=[
            pl.BlockSpec(memory_space=pltpu.SMEM),
            pl.BlockSpec((SEQ, D_SWA_Q), seq),
            pl.BlockSpec((SEQ, D_SWA_KV_DUP), seq),
            pl.BlockSpec((SEQ, D_SWA_KV_DUP), seq),
            pl.BlockSpec((SEQ, D_SB), seq),
            pl.BlockSpec((SEQ, D_SB), seq),
            pl.BlockSpec((SEQ, D_SB), seq),
            pl.BlockSpec((SB_KEY_TILE, SB_KEY_TILE), lambda b: (0, 0)),
        ],
        out_specs=[pl.BlockSpec((SEQ, D_SWA_Q), seq), pl.BlockSpec((SEQ, D_SB), seq)],
        out_shape=[jax.ShapeDtypeStruct((TOKENS, D_SWA_Q), BF16),
                   jax.ShapeDtypeStruct((TOKENS, D_SB), BF16)],
        scratch_shapes=[pltpu.VMEM((SB_HEADS, BLOCK, LANES), BF16),
                        pltpu.VMEM((SB_HEADS, BLOCK, LANES), F32),
                        pltpu.VMEM((SB_HEADS, BLOCK, LANES), F32),
                        pltpu.VMEM((SB_HEADS, BLOCK, SB_KEY_TILE), F32)],
        compiler_params=_params(1),
        name="attention",
    )(sinks, qa, ka, va, qs, ks, vs, ntri)


def _max4(a, b, c, d):
    return jnp.maximum(jnp.maximum(a, b), jnp.maximum(c, d))


def _first_index_of(vals, target):
    idx = jnp.full(target.shape, len(vals) - 1, jnp.int32)
    for p in range(len(vals) - 2, -1, -1):
        idx = jnp.where(vals[p] == target, p, idx)
    return idx


def _select_by_index(vals, idx):
    out = vals[-1]
    for p in range(len(vals) - 2, -1, -1):
        out = jnp.where(idx == p, vals[p], out)
    return out


def _mix_kernel(x_ref, ya_ref, yb_ref, wg_ref, bg_ref, wa_ref, wb_ref, wo_ref, lng_ref, lnb_ref,
                wrh_ref, wrl_ref, rb_ref, upper_ref,
                x1_ref, eid_ref, gate_ref, rank_ref, cnt_ref, base_ref):
    step = pl.program_id(0)
    x = x_ref[...]
    gates = jnp.dot(x.astype(BF16), wg_ref[...], preferred_element_type=F32) + bg_ref[...]
    pa = jnp.dot(ya_ref[...], wa_ref[...], preferred_element_type=F32)
    pb = jnp.dot(yb_ref[...], wb_ref[...], preferred_element_type=F32)
    merged = (jax.nn.sigmoid(gates[:, :D_MODEL]) * pa + jax.nn.sigmoid(gates[:, D_MODEL:]) * pb)
    mix = jnp.dot(merged.astype(BF16), wo_ref[...], preferred_element_type=F32)
    x1 = _layer_norm(DEEPNORM_ALPHA * x + mix, lng_ref[...], lnb_ref[...])
    _store_token_tiles(x1_ref, x1, ROW_BLOCK)

    x1h = x1.astype(BF16)
    x1l = (x1 - x1h.astype(F32)).astype(BF16)
    wrh = wrh_ref[...]
    logits = (lax.dot_general(wrh, x1h, NT_DIMS, preferred_element_type=F32)
              + lax.dot_general(wrh, x1l, NT_DIMS, preferred_element_type=F32)
              + lax.dot_general(wrl_ref[...], x1h, NT_DIMS, preferred_element_type=F32))
    aff = jax.nn.sigmoid(logits)
    reps = ROW_BLOCK // LANES
    biased = aff + jnp.concatenate([rb_ref[...]] * reps, axis=1)
    bm = [biased[p * N_GROUPS:(p + 1) * N_GROUPS, :] for p in range(EXPERTS_PER_GROUP)]
    am = [aff[p * N_GROUPS:(p + 1) * N_GROUPS, :] for p in range(EXPERTS_PER_GROUP)]
    hi1, lo1 = jnp.maximum(bm[0], bm[1]), jnp.minimum(bm[0], bm[1])
    hi2, lo2 = jnp.maximum(bm[2], bm[3]), jnp.minimum(bm[2], bm[3])
    group_score = (jnp.maximum(hi1, hi2)
                   + jnp.maximum(jnp.minimum(hi1, hi2), jnp.maximum(lo1, lo2)))
    group_iota = lax.broadcasted_iota(jnp.int32, (N_GROUPS, ROW_BLOCK), 0)
    best = jnp.max(group_score, axis=0, keepdims=True)
    g_sel = jnp.min(jnp.where(group_score == best, group_iota, N_GROUPS), axis=0, keepdims=True)
    in_sel = group_iota == g_sel
    bsel = [jnp.sum(jnp.where(in_sel, b, 0.0), axis=0, keepdims=True) for b in bm]
    asel = [jnp.sum(jnp.where(in_sel, a, 0.0), axis=0, keepdims=True) for a in am]
    i1 = _first_index_of(bsel, _max4(*bsel))
    rest = [jnp.where(i1 == p, -jnp.inf, bsel[p]) for p in range(EXPERTS_PER_GROUP)]
    i2 = _first_index_of(rest, _max4(*rest))
    a1 = _select_by_index(asel, i1)
    a2 = _select_by_index(asel, i2)
    e1 = g_sel * EXPERTS_PER_GROUP + i1
    e2 = g_sel * EXPERTS_PER_GROUP + i2
    eid_ref[...] = jnp.concatenate([e1, e2], axis=0)
    gate_ref[...] = jnp.concatenate([a1 / (a1 + a2), a2 / (a1 + a2)], axis=0)

    @pl.when(step == 0)
    def _():
        base_ref[...] = jnp.zeros_like(base_ref)

    expert_iota = lax.broadcasted_iota(jnp.int32, (N_EXPERTS, ROW_BLOCK), 0)
    oh1 = expert_iota == e1
    oh2 = expert_iota == e2
    upper = upper_ref[...]
    c1 = jnp.dot(jnp.where(oh1, 1.0, 0.0).astype(BF16), upper, preferred_element_type=F32)
    c2 = jnp.dot(jnp.where(oh2, 1.0, 0.0).astype(BF16), upper, preferred_element_type=F32)
    base = base_ref[...]
    tot1 = c1[:, ROW_BLOCK:]
    tot2 = c2[:, ROW_BLOCK:]
    before1 = jnp.concatenate([base] * reps, axis=1) + c1[:, :ROW_BLOCK]
    before2 = jnp.concatenate([base + tot1] * reps, axis=1) + c2[:, :ROW_BLOCK]
    r1 = jnp.sum(jnp.where(oh1, before1, 0.0), axis=0, keepdims=True)
    r2 = jnp.sum(jnp.where(oh2, before2, 0.0), axis=0, keepdims=True)
    rank_ref[...] = jnp.concatenate([r1, r2], axis=0).astype(jnp.int32)
    new_base = base + tot1 + tot2
    base_ref[...] = new_base
    cnt_ref[...] = new_base


def _mix(x2, ya, yb, lw, shared):
    row = lambda i: (i, 0)
    fixed = lambda i: (0, 0)
    col = lambda i: (0, i)
    full = lambda a: pl.BlockSpec(a.shape, fixed)
    weights = [lw["w_g"], lw["b_g"], lw["w_a"], lw["w_b"], lw["w_o"], lw["ln1_g"], lw["ln1_b"],
               shared["wr_hi"], shared["wr_lo"], shared["rbias"], shared["upper"]]
    return pl.pallas_call(
        _mix_kernel,
        grid=(TOKENS // ROW_BLOCK,),
        in_specs=[pl.BlockSpec((ROW_BLOCK, D_MODEL), row),
                  pl.BlockSpec((ROW_BLOCK, D_SWA_Q), row),
                  pl.BlockSpec((ROW_BLOCK, D_SB), row)] + [full(w) for w in weights],
        out_specs=[pl.BlockSpec((ROW_BLOCK * TILE_ROWS, LANES), row),
                   pl.BlockSpec((TOP_K, ROW_BLOCK), col),
                   pl.BlockSpec((TOP_K, ROW_BLOCK), col),
                   pl.BlockSpec((TOP_K, ROW_BLOCK), col),
                   pl.BlockSpec((N_EXPERTS, LANES), fixed)],
        out_shape=[jax.ShapeDtypeStruct((TOKENS * TILE_ROWS, LANES), F32),
                   jax.ShapeDtypeStruct((TOP_K, TOKENS), jnp.int32),
                   jax.ShapeDtypeStruct((TOP_K, TOKENS), F32),
                   jax.ShapeDtypeStruct((TOP_K, TOKENS), jnp.int32),
                   jax.ShapeDtypeStruct((N_EXPERTS, LANES), F32)],
        scratch_shapes=[pltpu.VMEM((N_EXPERTS, LANES), F32)],
        compiler_params=_params(1),
        name="mix_ln_route",
    )(x2, ya, yb, *weights)


def _token_tile(ref, row):
    return ref.at[pl.ds(pl.multiple_of(row * TILE_ROWS, TILE_ROWS), TILE_ROWS)]


def _dispatch_kernel(dest_ref, x1_ref, xs_in_hbm, xs_hbm, stage_ref, sems):
    del xs_in_hbm
    step = pl.program_id(0)
    n_steps = pl.num_programs(0)
    slot = step % 2
    stage_ref[slot] = x1_ref[...]

    def issue(t, c):
        for k in range(TOP_K):
            pltpu.make_async_copy(_token_tile(stage_ref.at[slot], t),
                                  _token_tile(xs_hbm, dest_ref[TOP_K * t + k]),
                                  sems.at[slot]).start(priority=k % 2)
        return c

    lax.fori_loop(0, DISPATCH_ROWS, issue, 0, unroll=8)

    def drain(slot_):
        def body(t, c):
            for k in range(TOP_K):
                pltpu.make_async_copy(_token_tile(stage_ref.at[slot_], 0), _token_tile(xs_hbm, 0),
                                      sems.at[slot_]).wait()
            return c
        lax.fori_loop(0, DISPATCH_ROWS, body, 0, unroll=8)

    @pl.when(step > 0)
    def _():
        drain(1 - slot)

    @pl.when(step == n_steps - 1)
    def _():
        drain(slot)


def _dispatch(dest, x1, xs):
    return pl.pallas_call(
        _dispatch_kernel,
        grid=(TOKENS // DISPATCH_ROWS,),
        in_specs=[pl.BlockSpec((TOP_K * DISPATCH_ROWS,), lambda i: (i,), memory_space=pltpu.SMEM),
                  pl.BlockSpec((DISPATCH_ROWS * TILE_ROWS, LANES), lambda i: (i, 0)),
                  pl.BlockSpec(memory_space=pl.ANY)],
        out_specs=pl.BlockSpec(memory_space=pl.ANY),
        out_shape=jax.ShapeDtypeStruct(xs.shape, xs.dtype),
        scratch_shapes=[pltpu.VMEM((2, DISPATCH_ROWS * TILE_ROWS, LANES), F32),
                        pltpu.SemaphoreType.DMA((2,))],
        input_output_aliases={2: 0},
        compiler_params=_params(1),
        name="dispatch",
    )(dest, x1, xs)


def _expert_kernel(be_ref, nu_ref, x_ref, wg_ref, wu_ref, wd_ref, y_ref, wg_bf, wu_bf, wd_bf):
    i = pl.program_id(0)

    @pl.when(i < nu_ref[0])
    def _():
        e = be_ref[i]
        e_prev = be_ref[jnp.maximum(i - 1, 0)]

        @pl.when((i == 0) | (e != e_prev))
        def _():
            wg_bf[...] = wg_ref[...].astype(BF16)
            wu_bf[...] = wu_ref[...].astype(BF16)
            wd_bf[...] = wd_ref[...].astype(BF16)

        xb = _load_token_tiles(x_ref, EXPERT_ROWS).astype(BF16)
        hg = jnp.dot(xb, wg_bf[...], preferred_element_type=F32)
        hu = jnp.dot(xb, wu_bf[...], preferred_element_type=F32)
        h = hg * jax.nn.sigmoid(hg) * hu
        y = jnp.dot(h.astype(BF16), wd_bf[...], preferred_element_type=F32)
        _store_token_tiles(y_ref, y, EXPERT_ROWS)

    @pl.when(i >= nu_ref[0])
    def _():
        y_ref[...] = jnp.zeros_like(y_ref)


def _experts(layer, block_expert, n_used, xs, w_gate, w_up, w_down):
    blk = lambda i, be, nu: (jnp.minimum(i, nu[0] - 1), 0)
    out_blk = lambda i, be, nu: (i, 0)
    wsel = lambda i, be, nu: (layer, be[jnp.minimum(i, nu[0] - 1)], 0, 0)
    grid_spec = pltpu.PrefetchScalarGridSpec(
        num_scalar_prefetch=2,
        grid=(N_EXPERT_BLOCKS,),
        in_specs=[pl.BlockSpec((EXPERT_ROWS * TILE_ROWS, LANES), blk),
                  pl.BlockSpec((None, None, D_MODEL, D_EXPERT), wsel),
                  pl.BlockSpec((None, None, D_MODEL, D_EXPERT), wsel),
                  pl.BlockSpec((None, None, D_EXPERT, D_MODEL), wsel)],
        out_specs=pl.BlockSpec((EXPERT_ROWS * TILE_ROWS, LANES), out_blk),
        scratch_shapes=[pltpu.VMEM((D_MODEL, D_EXPERT), BF16),
                        pltpu.VMEM((D_MODEL, D_EXPERT), BF16),
                        pltpu.VMEM((D_EXPERT, D_MODEL), BF16)],
    )
    return pl.pallas_call(
        _expert_kernel,
        grid_spec=grid_spec,
        out_shape=jax.ShapeDtypeStruct((N_EXPERT_BLOCKS * EXPERT_ROWS * TILE_ROWS, LANES), F32),
        compiler_params=_params(1),
        name="experts",
    )(block_expert, n_used, xs, w_gate, w_up, w_down)


def _combine_kernel(dest_ref, dest_next_ref, x1_ref, gate_ref, lng_ref, lnb_ref, ys_hbm,
                    x2_ref, buf_ref, sems):
    step = pl.program_id(0)
    n_steps = pl.num_programs(0)
    slot = step % 2

    def issue(d_ref, slot_):
        def body(t, c):
            for k in range(TOP_K):
                pltpu.make_async_copy(_token_tile(ys_hbm, d_ref[TOP_K * t + k]),
                                      _token_tile(buf_ref.at[slot_, k], t),
                                      sems.at[slot_]).start(priority=k % 2)
            return c
        lax.fori_loop(0, GATHER_ROWS, body, 0, unroll=8)

    @pl.when(step == 0)
    def _():
        issue(dest_ref, 0)

    @pl.when(step + 1 < n_steps)
    def _():
        issue(dest_next_ref, 1 - slot)

    def drain(t, c):
        for k in range(TOP_K):
            pltpu.make_async_copy(_token_tile(ys_hbm, 0), _token_tile(buf_ref.at[slot, k], 0),
                                  sems.at[slot]).wait()
        return c

    lax.fori_loop(0, GATHER_ROWS, drain, 0, unroll=8)
    g = gate_ref[...]
    y = (_load_token_tiles(buf_ref.at[slot, 0], GATHER_ROWS) * g[:, 0:1]
         + _load_token_tiles(buf_ref.at[slot, 1], GATHER_ROWS) * g[:, 1:2])
    x1 = _load_token_tiles(x1_ref, GATHER_ROWS)
    x2_ref[...] = _layer_norm(DEEPNORM_ALPHA * x1 + y, lng_ref[...], lnb_ref[...])


def _combine(dest, x1, gate_cols, ln_g, ln_b, ys):
    n_steps = TOKENS // GATHER_ROWS
    row = lambda i: (i, 0)
    fixed = lambda i: (0, 0)
    return pl.pallas_call(
        _combine_kernel,
        grid=(n_steps,),
        in_specs=[pl.BlockSpec((TOP_K * GATHER_ROWS,), lambda i: (i,), memory_space=pltpu.SMEM),
                  pl.BlockSpec((TOP_K * GATHER_ROWS,), lambda i: (jnp.minimum(i + 1, n_steps - 1),),
                               memory_space=pltpu.SMEM),
                  pl.BlockSpec((GATHER_ROWS * TILE_ROWS, LANES), row),
                  pl.BlockSpec((GATHER_ROWS, TOP_K), row),
                  pl.BlockSpec((1, D_MODEL), fixed),
                  pl.BlockSpec((1, D_MODEL), fixed),
                  pl.BlockSpec(memory_space=pl.ANY)],
        out_specs=pl.BlockSpec((GATHER_ROWS, D_MODEL), row),
        out_shape=jax.ShapeDtypeStruct((TOKENS, D_MODEL), F32),
        scratch_shapes=[pltpu.VMEM((2, TOP_K, GATHER_ROWS * TILE_ROWS, LANES), F32),
                        pltpu.SemaphoreType.DMA((2,))],
        compiler_params=_params(1),
        name="combine_ln",
    )(dest, dest, x1, gate_cols, ln_g, ln_b, ys)


def _prepare_layer(l, w_in, b_in, w_branch_a, w_branch_b, w_out, ln1_g, ln1_b):
    scale = HEAD_DIM ** -0.5

    def columns(t):
        head = lambda lo, h: t[..., lo + h * HEAD_DIM:lo + (h + 1) * HEAD_DIM]
        dup = lambda lo: [head(lo, h) for h in range(SWA_KV_HEADS) for _ in range(2)]
        return jnp.concatenate([t[..., 0:512] * scale] + dup(512) + dup(640)
                               + [t[..., 768:1280] * scale, t[..., 1280:2304]], axis=-1)

    w, b = w_in[l], b_in[l]
    return {
        "w_qkv": columns(w).astype(BF16),
        "b_qkv": columns(b)[None, :],
        "w_g": w[:, 2304:].astype(BF16),
        "b_g": b[None, 2304:],
        "w_a": w_branch_a[l].astype(BF16),
        "w_b": w_branch_b[l].astype(BF16),
        "w_o": w_out[l].astype(BF16),
        "ln1_g": ln1_g[l][None, :],
        "ln1_b": ln1_b[l][None, :],
    }


def _prepare_shared(w_router, router_bias):
    r = jnp.arange(N_EXPERTS)
    perm = (r % N_GROUPS) * EXPERTS_PER_GROUP + r // N_GROUPS
    wr_t = w_router.T[perm].astype(F32)
    wr_hi = wr_t.astype(BF16)
    wr_lo = (wr_t - wr_hi.astype(F32)).astype(BF16)
    rbias = jnp.broadcast_to(router_bias.astype(F32)[perm][:, None], (N_EXPERTS, LANES))
    t = jnp.arange(ROW_BLOCK)
    strict_upper = (t[:, None] < t[None, :]).astype(BF16)
    upper = jnp.concatenate([strict_upper, jnp.ones((ROW_BLOCK, LANES), BF16)], axis=1)
    k = jnp.arange(SB_KEY_TILE)
    ntri = -(k[:, None] >= k[None, :]).astype(BF16)
    return {"wr_hi": wr_hi, "wr_lo": wr_lo, "rbias": rbias, "upper": upper, "ntri": ntri}


def _block_plan(counts, eid, rank):
    n_blocks = (counts + EXPERT_ROWS - 1) // EXPERT_ROWS
    block_end = jnp.cumsum(n_blocks)
    row_start = (block_end - n_blocks) * EXPERT_ROWS
    n_used = block_end[-1:].astype(jnp.int32)
    blocks = jnp.arange(N_EXPERT_BLOCKS)
    block_expert = jnp.minimum(jnp.sum(block_end[None, :] <= blocks[:, None], axis=1),
                               N_EXPERTS - 1).astype(jnp.int32)
    experts = jnp.arange(N_EXPERTS)
    start_of = jnp.sum(jnp.where(eid[:, :, None] == experts, row_start, 0), axis=-1)
    dest = (start_of + rank).astype(jnp.int32)
    return block_expert, n_used, dest.T.reshape(-1)


def kernel(x, w_in, b_in, attn_sinks, w_branch_a, w_branch_b, w_out, ln1_g, ln1_b, w_router,
           router_bias, w_gate, w_up, w_down, ln2_g, ln2_b):
    shared = _prepare_shared(w_router, router_bias)
    x2 = x.reshape(TOKENS, D_MODEL)
    xs = jnp.zeros((N_EXPERT_BLOCKS * EXPERT_ROWS * TILE_ROWS, LANES), F32)
    for l in range(DEPTH):
        lw = _prepare_layer(l, w_in, b_in, w_branch_a, w_branch_b, w_out, ln1_g, ln1_b)
        qa, ka, va, qs, ks, vs = _inproj(x2, lw["w_qkv"], lw["b_qkv"])
        ya, yb = _attention(qa, ka, va, attn_sinks[l], qs, ks, vs, shared["ntri"])
        x1, eid, gate, rank, cnt = _mix(x2, ya, yb, lw, shared)
        block_expert, n_used, dest = _block_plan(cnt[:, 0].astype(jnp.int32), eid, rank)
        xs = _dispatch(dest, x1, xs)
        ys = _experts(l, block_expert, n_used, xs, w_gate, w_up, w_down)
        x2 = _combine(dest, x1, gate.T, ln2_g[l][None, :], ln2_b[l][None, :], ys)
    return x2.reshape(BATCH, SEQ, D_MODEL)
```
